```python
import jax, jax.numpy as jnp
from jax import lax
import numpy as np

D_MODEL = 1024
BATCH = 8
SEQ = 8192
DEPTH = 2

CHUNK = 64
N_MIXERS = 2
N_CONV_LAYERS = (DEPTH + 1) // 2
N_MLA_LAYERS = DEPTH // 2
D_FF = 2816
FFN_RES_WEIGHT = 0.5
CONV_WIDTH = 31
N_HEADS = 8
QK_NOPE = 128
QK_ROPE = 64
V_HEAD = 128
Q_LORA = 512
KV_LORA = 256
ROPE_THETA = 10000.0
Q_BLOCK = 128
RMS_EPS = 1e-6
POS_OFFSET_MAX = 65536

kernel_name = "hybrid_conformer_conv_mla_macaron"


def rmsnorm(x, g):
    xf = x.astype(jnp.float32)
    y = xf * lax.rsqrt(jnp.mean(xf * xf, axis=-1, keepdims=True) + RMS_EPS) * g.astype(jnp.float32)
    return y.astype(x.dtype)


def swiglu_ffn(h, w1, w3, w2):
    return (jax.nn.silu(h @ w1) * (h @ w3)) @ w2


def conv_module(h, w_pw1, w_dw, g_norm, w_pw2):
    u = h @ w_pw1
    a, b = jnp.split(u, 2, axis=-1)
    u = a * jax.nn.sigmoid(b)
    u = lax.conv_general_dilated(
        u, w_dw[:, None, :].astype(u.dtype), window_strides=(1,),
        padding=[(CONV_WIDTH - 1, 0)],
        dimension_numbers=("NWC", "WIO", "NWC"),
        feature_group_count=D_MODEL)
    u = jax.nn.silu(rmsnorm(u, g_norm))
    return u @ w_pw2


def apply_rope(x, cos, sin):
    half = x.shape[-1] // 2
    xf = x.astype(jnp.float32)
    x1, x2 = xf[..., :half], xf[..., half:]
    out = jnp.concatenate([x1 * cos - x2 * sin, x2 * cos + x1 * sin], axis=-1)
    return out.astype(x.dtype)


def mla(h, positions, w_a, g_q, g_kv, w_uq, w_ukv, w_o):
    a = h @ w_a
    c_q = rmsnorm(a[..., :Q_LORA], g_q)
    c_kv = rmsnorm(a[..., Q_LORA:Q_LORA + KV_LORA], g_kv)
    k_rope = a[..., Q_LORA + KV_LORA:]
    q = jnp.einsum("bsc,chd->bshd", c_q, w_uq)
    q_nope, q_rope = q[..., :QK_NOPE], q[..., QK_NOPE:]
    kv = jnp.einsum("bsc,chd->bshd", c_kv, w_ukv)
    k_nope, v = kv[..., :QK_NOPE], kv[..., QK_NOPE:]

    inv_freq = ROPE_THETA ** (-2.0 * jnp.arange(QK_ROPE // 2, dtype=jnp.float32) / QK_ROPE)
    ang = positions.astype(jnp.float32)[..., None] * inv_freq
    cos, sin = jnp.cos(ang), jnp.sin(ang)
    q_rope = apply_rope(q_rope, cos[:, :, None, :], sin[:, :, None, :])
    k_rope = apply_rope(k_rope, cos, sin)

    seq = h.shape[1]
    scale = (QK_NOPE + QK_ROPE) ** -0.5
    chunk_id = jnp.arange(seq) // CHUNK
    outs = []
    for blk in range(seq // Q_BLOCK):
        q0, q1 = blk * Q_BLOCK, (blk + 1) * Q_BLOCK
        s = (jnp.einsum("bqhd,bkhd->bhqk", q_nope[:, q0:q1], k_nope[:, :q1],
                        preferred_element_type=jnp.float32)
             + jnp.einsum("bqhr,bkr->bhqk", q_rope[:, q0:q1], k_rope[:, :q1],
                          preferred_element_type=jnp.float32)) * scale
        mask = chunk_id[q0:q1, None] >= chunk_id[None, :q1]
        s = jnp.where(mask[None, None], s, -jnp.inf)
        p = jax.nn.softmax(s, axis=-1).astype(v.dtype)
        outs.append(jnp.einsum("bhqk,bkhd->bqhd", p, v[:, :q1]))
    o = jnp.concatenate(outs, axis=1)
    return o.reshape(o.shape[0], seq, N_HEADS * V_HEAD) @ w_o


def setup_inputs(seed: int = 0) -> dict:
    key = jax.random.key(seed)
    ks = jax.random.split(key, 24)
    f32 = jnp.float32

    def w(k, shape, fan_in):
        return jax.random.normal(k, shape, f32) * (fan_in ** -0.5)

    def gain(k, shape):
        return 1.0 + 0.02 * jax.random.normal(k, shape, f32)

    x = jax.random.normal(ks[0], (BATCH, SEQ, D_MODEL), f32)
    offset = jax.random.randint(ks[1], (BATCH, 1), 0, POS_OFFSET_MAX, dtype=jnp.int32)
    positions = offset + jnp.arange(SEQ, dtype=jnp.int32)[None, :]
    return {
        "x": x,
        "positions": positions,
        "ffn_norm1": gain(ks[2], (DEPTH, D_MODEL)),
        "ffn1_w1": w(ks[3], (DEPTH, D_MODEL, D_FF), D_MODEL),
        "ffn1_w3": w(ks[4], (DEPTH, D_MODEL, D_FF), D_MODEL),
        "ffn1_w2": w(ks[5], (DEPTH, D_FF, D_MODEL), D_FF),
        "mix_norm": gain(ks[6], (DEPTH, D_MODEL)),
        "ffn_norm2": gain(ks[7], (DEPTH, D_MODEL)),
        "ffn2_w1": w(ks[8], (DEPTH, D_MODEL, D_FF), D_MODEL),
        "ffn2_w3": w(ks[9], (DEPTH, D_MODEL, D_FF), D_MODEL),
        "ffn2_w2": w(ks[10], (DEPTH, D_FF, D_MODEL), D_FF),
        "conv_w_pw1": w(ks[11], (N_CONV_LAYERS, D_MODEL, 2 * D_MODEL), D_MODEL),
        "conv_w_dw": w(ks[12], (N_CONV_LAYERS, CONV_WIDTH, D_MODEL), CONV_WIDTH),
        "conv_norm": gain(ks[13], (N_CONV_LAYERS, D_MODEL)),
        "conv_w_pw2": w(ks[14], (N_CONV_LAYERS, D_MODEL, D_MODEL), D_MODEL),
        "mla_w_a": w(ks[15], (N_MLA_LAYERS, D_MODEL, Q_LORA + KV_LORA + QK_ROPE), D_MODEL),
        "mla_q_norm": gain(ks[16], (N_MLA_LAYERS, Q_LORA)),
        "mla_kv_norm": gain(ks[17], (N_MLA_LAYERS, KV_LORA)),
        "mla_w_uq": w(ks[18], (N_MLA_LAYERS, Q_LORA, N_HEADS, QK_NOPE + QK_ROPE), Q_LORA),
        "mla_w_ukv": w(ks[19], (N_MLA_LAYERS, KV_LORA, N_HEADS, QK_NOPE + V_HEAD), KV_LORA),
        "mla_w_o": w(ks[20], (N_MLA_LAYERS, N_HEADS * V_HEAD, D_MODEL), N_HEADS * V_HEAD),
        "final_norm": gain(ks[21], (D_MODEL,)),
    }


def reference(x, positions, ffn_norm1, ffn1_w1, ffn1_w3, ffn1_w2, mix_norm, ffn_norm2,
              ffn2_w1, ffn2_w3, ffn2_w2, conv_w_pw1, conv_w_dw, conv_norm, conv_w_pw2,
              mla_w_a, mla_q_norm, mla_kv_norm, mla_w_uq, mla_w_ukv, mla_w_o, final_norm):
    h = x
    for i in range(DEPTH):
        h = h + FFN_RES_WEIGHT * swiglu_ffn(rmsnorm(h, ffn_norm1[i]), ffn1_w1[i], ffn1_w3[i], ffn1_w2[i])
        m = rmsnorm(h, mix_norm[i])
        j = i // N_MIXERS
        if i % N_MIXERS == 0:
            h = h + conv_module(m, conv_w_pw1[j], conv_w_dw[j], conv_norm[j], conv_w_pw2[j])
        else:
            h = h + mla(m, positions, mla_w_a[j], mla_q_norm[j], mla_kv_norm[j],
                        mla_w_uq[j], mla_w_ukv[j], mla_w_o[j])
        h = h + FFN_RES_WEIGHT * swiglu_ffn(rmsnorm(h, ffn_norm2[i]), ffn2_w1[i], ffn2_w3[i], ffn2_w2[i])
    return rmsnorm(h, final_norm)
```

```python
import functools

import jax
import jax.numpy as jnp
from jax import lax
from jax.experimental import pallas as pl
from jax.experimental.pallas import tpu as pltpu

CHUNK = 64
CONV_WIDTH = 31
N_HEADS = 8
QK_NOPE = 128
QK_ROPE = 64
V_HEAD = 128
Q_LORA = 512
KV_LORA = 256
ROPE_THETA = 10000.0
RMS_EPS = 1e-6
FFN_RES_WEIGHT = 0.5

LANES = 128
SUBLANES = 8
VMEM_LIMIT_BYTES = 56 * 1024 * 1024

FFN_ROWS = 512
CONV_ROWS = 512
CONV_TAIL = 32
CONV_ROW_BLOCK = 64
PROJ_ROWS = 512
ATTN_BLOCK = 512
HEAD_PAD = 2 * LANES

BF16 = jnp.bfloat16
F32 = jnp.float32


def _rms(x, g):
    return x * lax.rsqrt(jnp.mean(x * x, axis=-1, keepdims=True) + RMS_EPS) * g


def _dot(a, b):
    return jnp.dot(a, b, preferred_element_type=F32)


def _resident(shape):
    zeros = (0,) * len(shape)
    return pl.BlockSpec(shape, lambda *_: zeros, pipeline_mode=pl.Buffered(1))


def _params(n_axes):
    return pltpu.CompilerParams(
        dimension_semantics=("arbitrary",) * n_axes,
        vmem_limit_bytes=VMEM_LIMIT_BYTES)


def _ffn_body(h, g, w1_ref, w3_ref, w2_ref):
    x = _rms(h, g).astype(BF16)
    gate = _dot(x, w1_ref[...])
    up = _dot(x, w3_ref[...])
    a = (gate * jax.nn.sigmoid(gate) * up).astype(BF16)
    return h + FFN_RES_WEIGHT * _dot(a, w2_ref[...])


def _ffn_kernel(h_ref, g_ref, w1_ref, w3_ref, w2_ref, o_ref):
    o_ref[...] = _ffn_body(h_ref[...], g_ref[...], w1_ref, w3_ref, w2_ref)


def _ffn_final_kernel(h_ref, g_ref, w1_ref, w3_ref, w2_ref, gf_ref, o_ref):
    out = _ffn_body(h_ref[...], g_ref[...], w1_ref, w3_ref, w2_ref)
    o_ref[...] = _rms(out, gf_ref[...])


def _ffn(h, g, w1, w3, w2, final_gain=None):
    t, d = h.shape
    d_ff = w1.shape[1]
    row = pl.BlockSpec((FFN_ROWS, d), lambda i: (i, 0))
    in_specs = [row, _resident((1, d)), _resident((d, d_ff)), _resident((d, d_ff)),
                _resident((d_ff, d))]
    args = [h, g.reshape(1, d), w1, w3, w2]
    body = _ffn_kernel
    if final_gain is not None:
        in_specs.append(_resident((1, d)))
        args.append(final_gain.reshape(1, d))
        body = _ffn_final_kernel
    return pl.pallas_call(
        body,
        grid=(t // FFN_ROWS,),
        in_specs=in_specs,
        out_specs=row,
        out_shape=jax.ShapeDtypeStruct((t, d), F32),
        compiler_params=_params(1),
        name="ffn_final" if final_gain is not None else "ffn",
    )(*args)


def _conv_kernel(h_ref, g_ref, wpw1_ref, wdw_ref, gn_ref, wpw2_ref, o_ref, u_ref, y_ref):
    rows, d = y_ref.shape
    n_lane_tiles = d // LANES

    @pl.when(pl.program_id(1) == 0)
    def _():
        u_ref[:, 0:CONV_TAIL, :] = jnp.zeros((n_lane_tiles, CONV_TAIL, LANES), F32)

    h = h_ref[0]
    m = _rms(h, g_ref[...]).astype(BF16)
    u = _dot(m, wpw1_ref[...])
    glu = u[:, :d] * jax.nn.sigmoid(u[:, d:])
    for c in range(n_lane_tiles):
        u_ref[c, CONV_TAIL:CONV_TAIL + rows, :] = glu[:, c * LANES:(c + 1) * LANES]

    first = CONV_TAIL - (CONV_WIDTH - 1)
    for c in range(n_lane_tiles):
        lanes = slice(c * LANES, (c + 1) * LANES)
        taps = [wdw_ref[k:k + 1, lanes] for k in range(CONV_WIDTH)]

        def row_block(rb, carry, c=c, lanes=lanes, taps=taps):
            r0 = pl.multiple_of(rb * CONV_ROW_BLOCK, CONV_ROW_BLOCK)
            acc = jnp.zeros((CONV_ROW_BLOCK, LANES), F32)
            for k in range(CONV_WIDTH):
                acc = acc + taps[k] * u_ref[c, pl.ds(r0 + first + k, CONV_ROW_BLOCK), :]
            y_ref[pl.ds(r0, CONV_ROW_BLOCK), lanes] = acc
            return carry

        lax.fori_loop(0, rows // CONV_ROW_BLOCK, row_block, 0)

    u_ref[:, 0:CONV_TAIL, :] = u_ref[:, rows:rows + CONV_TAIL, :]

    z = _rms(y_ref[...], gn_ref[...])
    z = (z * jax.nn.sigmoid(z)).astype(BF16)
    o_ref[0] = h + _dot(z, wpw2_ref[...])


def _conv_module(h, g, w_pw1, w_dw, g_norm, w_pw2):
    b, s, d = h.shape
    row = pl.BlockSpec((1, CONV_ROWS, d), lambda bi, si: (bi, si, 0))
    return pl.pallas_call(
        _conv_kernel,
        grid=(b, s // CONV_ROWS),
        in_specs=[row, _resident((1, d)), _resident((d, 2 * d)), _resident((CONV_WIDTH, d)),
                  _resident((1, d)), _resident((d, d))],
        out_specs=row,
        out_shape=jax.ShapeDtypeStruct((b, s, d), F32),
        scratch_shapes=[pltpu.VMEM((d // LANES, CONV_TAIL + CONV_ROWS, LANES), F32),
                        pltpu.VMEM((CONV_ROWS, d), F32)],
        compiler_params=_params(2),
        name="conv_module",
    )(h, g.reshape(1, d), w_pw1, w_dw, g_norm.reshape(1, d), w_pw2)


def _mla_proj_kernel(h_ref, pos_ref, freq_ref, g_ref, wa_ref, gq_ref, gkv_ref, wqn_ref, wqr_ref, wqs_ref,
                     wk_ref, wv_ref, q_ref, k_ref, v_ref):
    m = _rms(h_ref[0], g_ref[...]).astype(BF16)
    a = _dot(m, wa_ref[...])
    c_q = _rms(a[:, :Q_LORA], gq_ref[...]).astype(BF16)
    c_kv = _rms(a[:, Q_LORA:Q_LORA + KV_LORA], gkv_ref[...]).astype(BF16)
    kr = a[:, Q_LORA + KV_LORA:Q_LORA + KV_LORA + LANES]
    kr_swapped = a[:, Q_LORA + KV_LORA + LANES:]

    half = QK_ROPE // 2
    lane = lax.broadcasted_iota(jnp.int32, (1, LANES), 1)
    ang = pos_ref[0] * freq_ref[...]
    in_rope = lane < QK_ROPE
    cos_t = jnp.where(in_rope, jnp.cos(ang), 0.0)
    sin_t = jnp.where(lane < half, -jnp.sin(ang), jnp.where(in_rope, jnp.sin(ang), 0.0))

    scale = (QK_NOPE + QK_ROPE) ** -0.5
    q_nope = _dot(c_q, wqn_ref[...])
    q_rope = _dot(c_q, wqr_ref[...])
    q_rope_swapped = _dot(c_q, wqs_ref[...])
    k_nope = _dot(c_kv, wk_ref[...])
    v = _dot(c_kv, wv_ref[...])
    k_rot = (kr * cos_t + kr_swapped * sin_t).astype(BF16)
    for hd in range(N_HEADS):
        lanes = slice(hd * LANES, (hd + 1) * LANES)
        q_rot = q_rope[:, lanes] * cos_t + q_rope_swapped[:, lanes] * sin_t
        q_ref[0, hd, :, 0:LANES] = (q_nope[:, lanes] * scale).astype(BF16)
        q_ref[0, hd, :, LANES:HEAD_PAD] = (q_rot * scale).astype(BF16)
        k_ref[0, hd, :, 0:LANES] = k_nope[:, lanes].astype(BF16)
        k_ref[0, hd, :, LANES:HEAD_PAD] = k_rot
        v_ref[0, hd] = v[:, lanes].astype(BF16)


def _rope_tile_weights(w_rope):
    c, g, _ = w_rope.shape
    half = QK_ROPE // 2
    x1, x2 = w_rope[..., :half], w_rope[..., half:]
    pad = jnp.zeros((c, g, LANES - QK_ROPE), w_rope.dtype)
    plain = jnp.concatenate([x1, x2, pad], axis=-1).reshape(c, g * LANES)
    swapped = jnp.concatenate([x2, x1, pad], axis=-1).reshape(c, g * LANES)
    return plain, swapped


def _mla_proj(h, pos, g, w_a, g_q, g_kv, w_uq, w_ukv):
    b, s, d = h.shape
    kr_plain, kr_swapped = _rope_tile_weights(w_a[:, None, Q_LORA + KV_LORA:])
    wa = jnp.concatenate([w_a[:, :Q_LORA + KV_LORA], kr_plain, kr_swapped], axis=1).astype(BF16)
    wqn = w_uq[:, :, :QK_NOPE].reshape(Q_LORA, N_HEADS * QK_NOPE).astype(BF16)
    wqr, wqs = _rope_tile_weights(w_uq[:, :, QK_NOPE:])
    wk = w_ukv[:, :, :QK_NOPE].reshape(KV_LORA, N_HEADS * QK_NOPE).astype(BF16)
    wv = w_ukv[:, :, QK_NOPE:].reshape(KV_LORA, N_HEADS * V_HEAD).astype(BF16)
    inv_freq = ROPE_THETA ** (-2.0 * jnp.arange(QK_ROPE // 2, dtype=F32) / QK_ROPE)
    freq_tile = jnp.tile(inv_freq, LANES // (QK_ROPE // 2)).reshape(1, LANES)

    row = pl.BlockSpec((1, PROJ_ROWS, d), lambda bi, si: (bi, si, 0))
    head_rows = lambda width: pl.BlockSpec((1, N_HEADS, PROJ_ROWS, width),
                                           lambda bi, si: (bi, 0, si, 0))
    return pl.pallas_call(
        _mla_proj_kernel,
        grid=(b, s // PROJ_ROWS),
        in_specs=[row, pl.BlockSpec((1, PROJ_ROWS, 1), lambda bi, si: (bi, si, 0)),
                  _resident((1, LANES)), _resident((1, d)), _resident(wa.shape),
                  _resident((1, Q_LORA)),
                  _resident((1, KV_LORA)), _resident(wqn.shape), _resident(wqr.shape),
                  _resident(wqs.shape), _resident(wk.shape), _resident(wv.shape)],
        out_specs=[head_rows(HEAD_PAD), head_rows(HEAD_PAD), head_rows(V_HEAD)],
        out_shape=[jax.ShapeDtypeStruct((b, N_HEADS, s, HEAD_PAD), BF16),
                   jax.ShapeDtypeStruct((b, N_HEADS, s, HEAD_PAD), BF16),
                   jax.ShapeDtypeStruct((b, N_HEADS, s, V_HEAD), BF16)],
        compiler_params=_params(2),
        name="mla_proj",
    )(h, pos.astype(F32)[..., None], freq_tile, g.reshape(1, d), wa, g_q.reshape(1, Q_LORA),
      g_kv.reshape(1, KV_LORA), wqn, wqr.astype(BF16), wqs.astype(BF16), wk, wv)


def _attn_kernel(q_ref, k_ref, v_ref, o_ref):
    blk = ATTN_BLOCK
    qi = pl.program_id(2)
    q = q_ref[0, 0]

    def scores(j):
        k = k_ref[0, 0, pl.ds(pl.multiple_of(j * blk, blk), blk), :]
        return lax.dot_general(q, k, (((1,), (1,)), ((), ())), preferred_element_type=F32)

    def values(j):
        return v_ref[0, 0, pl.ds(pl.multiple_of(j * blk, blk), blk), :]

    row_chunk = lax.broadcasted_iota(jnp.int32, (blk, blk), 0) // CHUNK
    col_chunk = lax.broadcasted_iota(jnp.int32, (blk, blk), 1) // CHUNK
    s = jnp.where(row_chunk >= col_chunk, scores(qi), -jnp.inf)
    m = jnp.max(s, axis=1, keepdims=True)
    p = jnp.exp(s - m)
    l = jnp.sum(p, axis=1, keepdims=True)
    acc = _dot(p.astype(BF16), values(qi))

    def step(j, carry):
        m, l, acc = carry
        s = scores(j)
        m_new = jnp.maximum(m, jnp.max(s, axis=1, keepdims=True))
        alpha = jnp.exp(m - m_new)
        p = jnp.exp(s - m_new)
        l = alpha * l + jnp.sum(p, axis=1, keepdims=True)
        acc = alpha * acc + _dot(p.astype(BF16), values(j))
        return m_new, l, acc

    m, l, acc = lax.fori_loop(0, qi, step, (m, l, acc))
    o_ref[0] = (acc / l).astype(BF16)


def _attention(q, k, v):
    b, n_heads, s, _ = q.shape
    whole = lambda width: pl.BlockSpec((1, 1, s, width), lambda bi, hi, qi: (bi, hi, 0, 0))
    return pl.pallas_call(
        _attn_kernel,
        grid=(b, n_heads, s // ATTN_BLOCK),
        in_specs=[pl.BlockSpec((1, 1, ATTN_BLOCK, HEAD_PAD), lambda bi, hi, qi: (bi, hi, qi, 0)),
                  whole(HEAD_PAD), whole(V_HEAD)],
        out_specs=pl.BlockSpec((1, ATTN_BLOCK, V_HEAD), lambda bi, hi, qi: (bi, qi, hi)),
        out_shape=jax.ShapeDtypeStruct((b, s, n_heads * V_HEAD), BF16),
        compiler_params=_params(3),
        name="mla_attention",
    )(q, k, v)


def _out_proj_kernel(h_ref, o_ref, w_ref, out_ref):
    out_ref[...] = h_ref[...] + _dot(o_ref[...], w_ref[...])


def _out_proj(h, o, w_o):
    t, d = h.shape
    row = lambda width: pl.BlockSpec((FFN_ROWS, width), lambda i: (i, 0))
    return pl.pallas_call(
        _out_proj_kernel,
        grid=(t // FFN_ROWS,),
        in_specs=[row(d), row(o.shape[1]), _resident(w_o.shape)],
        out_specs=row(d),
        out_shape=jax.ShapeDtypeStruct((t, d), F32),
        compiler_params=_params(1),
        name="attn_out_proj",
    )(h, o, w_o)


def kernel(x, positions, ffn_norm1, ffn1_w1, ffn1_w3, ffn1_w2, mix_norm, ffn_norm2, ffn2_w1, ffn2_w3, ffn2_w2, conv_w_pw1, conv_w_dw, conv_norm, conv_w_pw2, mla_w_a, mla_q_norm, mla_kv_norm, mla_w_uq, mla_w_ukv, mla_w_o, final_norm):
    b, s, d = x.shape
    depth = ffn_norm1.shape[0]
    n_mixers = 2
    bf = lambda w: w.astype(BF16)
    h = x
    for i in range(depth):
        h = _ffn(h.reshape(b * s, d), ffn_norm1[i], bf(ffn1_w1[i]), bf(ffn1_w3[i]),
                 bf(ffn1_w2[i])).reshape(b, s, d)
        j = i // n_mixers
        if i % n_mixers == 0:
            h = _conv_module(h, mix_norm[i], bf(conv_w_pw1[j]), conv_w_dw[j], conv_norm[j],
                             bf(conv_w_pw2[j]))
        else:
            q, k, v = _mla_proj(h, positions, mix_norm[i], mla_w_a[j], mla_q_norm[j],
                                mla_kv_norm[j], mla_w_uq[j], mla_w_ukv[j])
            o = _attention(q, k, v)
            h = _out_proj(h.reshape(b * s, d), o.reshape(b * s, d), bf(mla_w_o[j])).reshape(b, s, d)
        last = i == depth - 1
        h = _ffn(h.reshape(b * s, d), ffn_norm2[i], bf(ffn2_w1[i]), bf(ffn2_w3[i]),
                 bf(ffn2_w2[i]), final_gain=final_norm if last else None).reshape(b, s, d)
    return h
```

```python
import functools

import jax
import jax.numpy as jnp
from jax import lax
from jax.experimental import pallas as pl
from jax.experimental.pallas import tpu as pltpu

CHUNK = 64
CONV_WIDTH = 31
N_HEADS = 8
QK_NOPE = 128
QK_ROPE = 64
V_HEAD = 128
Q_LORA = 512
KV_LORA = 256
ROPE_THETA = 10000.0
RMS_EPS = 1e-6
FFN_RES_WEIGHT = 0.5
LOG2_E = 1.4426950408889634

LANES = 128
SUBLANES = 8
VMEM_LIMIT_BYTES = 56 * 1024 * 1024

FFN_ROWS = 512
CONV_ROWS = 512
CONV_TAIL = 32
CONV_ROW_BLOCK = 64
PROJ_ROWS = 512
ATTN_BLOCK = 512
ATTN_Q_BLOCK = 1024
HEAD_PAD = 2 * LANES

BF16 = jnp.bfloat16
F32 = jnp.float32


def _rms(x, g):
    return x * lax.rsqrt(jnp.mean(x * x, axis=-1, keepdims=True) + RMS_EPS) * g


def _dot(a, b):
    return jnp.dot(a, b, preferred_element_type=F32)


def _resident(shape):
    zeros = (0,) * len(shape)
    return pl.BlockSpec(shape, lambda *_: zeros, pipeline_mode=pl.Buffered(1))


def _params(n_axes):
    return pltpu.CompilerParams(
        dimension_semantics=("arbitrary",) * n_axes,
        vmem_limit_bytes=VMEM_LIMIT_BYTES)


def _ffn_body(h, g, w1_ref, w3_ref, w2_ref):
    x = _rms(h, g).astype(BF16)
    gate = _dot(x, w1_ref[...])
    up = _dot(x, w3_ref[...])
    a = (gate * jax.nn.sigmoid(gate) * up).astype(BF16)
    return h + FFN_RES_WEIGHT * _dot(a, w2_ref[...])


def _ffn_kernel(h_ref, g_ref, w1_ref, w3_ref, w2_ref, o_ref):
    o_ref[...] = _ffn_body(h_ref[...], g_ref[...], w1_ref, w3_ref, w2_ref)


def _ffn_final_kernel(h_ref, g_ref, w1_ref, w3_ref, w2_ref, gf_ref, o_ref):
    out = _ffn_body(h_ref[...], g_ref[...], w1_ref, w3_ref, w2_ref)
    o_ref[...] = _rms(out, gf_ref[...])


def _ffn(h, g, w1, w3, w2, final_gain=None):
    t, d = h.shape
    d_ff = w1.shape[1]
    row = pl.BlockSpec((FFN_ROWS, d), lambda i: (i, 0))
    in_specs = [row, _resident((1, d)), _resident((d, d_ff)), _resident((d, d_ff)),
                _resident((d_ff, d))]
    args = [h, g.reshape(1, d), w1, w3, w2]
    body = _ffn_kernel
    if final_gain is not None:
        in_specs.append(_resident((1, d)))
        args.append(final_gain.reshape(1, d))
        body = _ffn_final_kernel
    return pl.pallas_call(
        body,
        grid=(t // FFN_ROWS,),
        in_specs=in_specs,
        out_specs=row,
        out_shape=jax.ShapeDtypeStruct((t, d), F32),
        compiler_params=_params(1),
        name="ffn_final" if final_gain is not None else "ffn",
    )(*args)


def _conv_kernel(h_ref, g_ref, wpw1_ref, wdw_ref, gn_ref, wpw2_ref, o_ref, u_ref, y_ref):
    rows, d = y_ref.shape
    n_lane_tiles = d // LANES

    @pl.when(pl.program_id(1) == 0)
    def _():
        u_ref[:, 0:CONV_TAIL, :] = jnp.zeros((n_lane_tiles, CONV_TAIL, LANES), F32)

    h = h_ref[0]
    m = _rms(h, g_ref[...]).astype(BF16)
    u = _dot(m, wpw1_ref[...])
    glu = u[:, :d] * jax.nn.sigmoid(u[:, d:])
    for c in range(n_lane_tiles):
        u_ref[c, CONV_TAIL:CONV_TAIL + rows, :] = glu[:, c * LANES:(c + 1) * LANES]

    first = CONV_TAIL - (CONV_WIDTH - 1)
    for c in range(n_lane_tiles):
        lanes = slice(c * LANES, (c + 1) * LANES)
        taps = [wdw_ref[k:k + 1, lanes] for k in range(CONV_WIDTH)]

        def row_block(rb, carry, c=c, lanes=lanes, taps=taps):
            r0 = pl.multiple_of(rb * CONV_ROW_BLOCK, CONV_ROW_BLOCK)
            acc = jnp.zeros((CONV_ROW_BLOCK, LANES), F32)
            for k in range(CONV_WIDTH):
                acc = acc + taps[k] * u_ref[c, pl.ds(r0 + first + k, CONV_ROW_BLOCK), :]
            y_ref[pl.ds(r0, CONV_ROW_BLOCK), lanes] = acc
            return carry

        lax.fori_loop(0, rows // CONV_ROW_BLOCK, row_block, 0)

    u_ref[:, 0:CONV_TAIL, :] = u_ref[:, rows:rows + CONV_TAIL, :]

    z = _rms(y_ref[...], gn_ref[...])
    z = (z * jax.nn.sigmoid(z)).astype(BF16)
    o_ref[0] = h + _dot(z, wpw2_ref[...])


def _conv_module(h, g, w_pw1, w_dw, g_norm, w_pw2):
    b, s, d = h.shape
    row = pl.BlockSpec((1, CONV_ROWS, d), lambda bi, si: (bi, si, 0))
    return pl.pallas_call(
        _conv_kernel,
        grid=(b, s // CONV_ROWS),
        in_specs=[row, _resident((1, d)), _resident((d, 2 * d)), _resident((CONV_WIDTH, d)),
                  _resident((1, d)), _resident((d, d))],
        out_specs=row,
        out_shape=jax.ShapeDtypeStruct((b, s, d), F32),
        scratch_shapes=[pltpu.VMEM((d // LANES, CONV_TAIL + CONV_ROWS, LANES), F32),
                        pltpu.VMEM((CONV_ROWS, d), F32)],
        compiler_params=_params(2),
        name="conv_module",
    )(h, g.reshape(1, d), w_pw1, w_dw, g_norm.reshape(1, d), w_pw2)


def _rope_tables(ang, index):
    half = QK_ROPE // 2
    in_rope = index < QK_ROPE
    cos_t = jnp.where(in_rope, jnp.cos(ang), 0.0)
    sin_t = jnp.where(index < half, -jnp.sin(ang), jnp.where(in_rope, jnp.sin(ang), 0.0))
    return cos_t, sin_t


def _dot_nt(a, b):
    return lax.dot_general(a, b, (((1,), (1,)), ((), ())), preferred_element_type=F32)


def _mla_proj_kernel(h_ref, pos_col_ref, pos_row_ref, freq_row_ref, freq_col_ref, g_ref, wa_ref,
                     gq_ref, gkv_ref, wqn_ref, wqr_ref, wqs_ref, wk_ref, wv_ref,
                     qt_ref, k_ref, vt_ref):
    m = _rms(h_ref[0], g_ref[...]).astype(BF16)
    a = _dot(m, wa_ref[...])
    c_q = _rms(a[:, :Q_LORA], gq_ref[...]).astype(BF16)
    c_kv = _rms(a[:, Q_LORA:Q_LORA + KV_LORA], gkv_ref[...]).astype(BF16)
    kr = a[:, Q_LORA + KV_LORA:Q_LORA + KV_LORA + LANES]
    kr_swapped = a[:, Q_LORA + KV_LORA + LANES:]

    cos_k, sin_k = _rope_tables(pos_col_ref[0] * freq_row_ref[...],
                                lax.broadcasted_iota(jnp.int32, (1, LANES), 1))
    cos_q, sin_q = _rope_tables(freq_col_ref[...] * pos_row_ref[0],
                                lax.broadcasted_iota(jnp.int32, (LANES, 1), 0))

    scale = (QK_NOPE + QK_ROPE) ** -0.5 * LOG2_E
    qt_nope = _dot_nt(wqn_ref[...], c_q)
    qt_rope = _dot_nt(wqr_ref[...], c_q)
    qt_rope_swapped = _dot_nt(wqs_ref[...], c_q)
    k_nope = _dot(c_kv, wk_ref[...])
    vt = _dot_nt(wv_ref[...], c_kv)
    k_rot = (kr * cos_k + kr_swapped * sin_k).astype(BF16)
    for hd in range(N_HEADS):
        sl = slice(hd * LANES, (hd + 1) * LANES)
        qt_rot = qt_rope[sl, :] * cos_q + qt_rope_swapped[sl, :] * sin_q
        qt_ref[0, hd, 0:LANES, :] = (qt_nope[sl, :] * scale).astype(BF16)
        qt_ref[0, hd, LANES:HEAD_PAD, :] = (qt_rot * scale).astype(BF16)
        k_ref[0, hd, :, 0:LANES] = k_nope[:, sl].astype(BF16)
        k_ref[0, hd, :, LANES:HEAD_PAD] = k_rot
        vt_ref[0, hd, 0] = vt[sl, :].astype(BF16)


def _rope_tile_weights(w_rope):
    c, g, _ = w_rope.shape
    half = QK_ROPE // 2
    x1, x2 = w_rope[..., :half], w_rope[..., half:]
    pad = jnp.zeros((c, g, LANES - QK_ROPE), w_rope.dtype)
    plain = jnp.concatenate([x1, x2, pad], axis=-1).reshape(c, g * LANES)
    swapped = jnp.concatenate([x2, x1, pad], axis=-1).reshape(c, g * LANES)
    return plain, swapped


def _mla_proj(h, pos, g, w_a, g_q, g_kv, w_uq, w_ukv):
    b, s, d = h.shape
    kr_plain, kr_swapped = _rope_tile_weights(w_a[:, None, Q_LORA + KV_LORA:])
    wa = jnp.concatenate([w_a[:, :Q_LORA + KV_LORA], kr_plain, kr_swapped], axis=1).astype(BF16)
    wqn_t = w_uq[:, :, :QK_NOPE].reshape(Q_LORA, N_HEADS * QK_NOPE).T.astype(BF16)
    wqr, wqs = _rope_tile_weights(w_uq[:, :, QK_NOPE:])
    wqr_t, wqs_t = wqr.T.astype(BF16), wqs.T.astype(BF16)
    wk = w_ukv[:, :, :QK_NOPE].reshape(KV_LORA, N_HEADS * QK_NOPE).astype(BF16)
    wv_t = w_ukv[:, :, QK_NOPE:].reshape(KV_LORA, N_HEADS * V_HEAD).T.astype(BF16)
    inv_freq = ROPE_THETA ** (-2.0 * jnp.arange(QK_ROPE // 2, dtype=F32) / QK_ROPE)
    freq_tile = jnp.tile(inv_freq, LANES // (QK_ROPE // 2))
    pos_f = pos.astype(F32)

    assert PROJ_ROWS == ATTN_BLOCK
    row = pl.BlockSpec((1, PROJ_ROWS, d), lambda bi, si: (bi, si, 0))
    return pl.pallas_call(
        _mla_proj_kernel,
        grid=(b, s // PROJ_ROWS),
        in_specs=[row,
                  pl.BlockSpec((1, PROJ_ROWS, 1), lambda bi, si: (bi, si, 0)),
                  pl.BlockSpec((1, 1, PROJ_ROWS), lambda bi, si: (bi, 0, si)),
                  _resident((1, LANES)), _resident((LANES, 1)), _resident((1, d)),
                  _resident(wa.shape), _resident((1, Q_LORA)), _resident((1, KV_LORA)),
                  _resident(wqn_t.shape), _resident(wqr_t.shape), _resident(wqs_t.shape),
                  _resident(wk.shape), _resident(wv_t.shape)],
        out_specs=[pl.BlockSpec((1, N_HEADS, HEAD_PAD, PROJ_ROWS), lambda bi, si: (bi, 0, 0, si)),
                   pl.BlockSpec((1, N_HEADS, PROJ_ROWS, HEAD_PAD), lambda bi, si: (bi, 0, si, 0)),
                   pl.BlockSpec((1, N_HEADS, 1, V_HEAD, PROJ_ROWS),
                                lambda bi, si: (bi, 0, si, 0, 0))],
        out_shape=[jax.ShapeDtypeStruct((b, N_HEADS, HEAD_PAD, s), BF16),
                   jax.ShapeDtypeStruct((b, N_HEADS, s, HEAD_PAD), BF16),
                   jax.ShapeDtypeStruct((b, N_HEADS, s // PROJ_ROWS, V_HEAD, PROJ_ROWS), BF16)],
        compiler_params=_params(2),
        name="mla_proj",
    )(h, pos_f[:, :, None], pos_f[:, None, :], freq_tile.reshape(1, LANES),
      freq_tile.reshape(LANES, 1), g.reshape(1, d), wa, g_q.reshape(1, Q_LORA),
      g_kv.reshape(1, KV_LORA), wqn_t, wqr_t, wqs_t, wk, wv_t)


def _attn_kernel(qt_ref, k_ref, vt_ref, o_ref, s0_ref, s1_ref, acc_ref):
    qb, kb = ATTN_Q_BLOCK, ATTN_BLOCK
    blocks_per_q = qb // kb
    qi = pl.program_id(2)

    def raw_scores(j, s_ref):
        k = k_ref[0, 0, pl.ds(pl.multiple_of(j * kb, kb), kb), :]
        s_ref[...] = _dot(k, qt_ref[0, 0])
        return jnp.max(s_ref[...], axis=0, keepdims=True)

    def mask_scores(j, s_ref):
        key_chunk = (j * kb + lax.broadcasted_iota(jnp.int32, (kb, qb), 0)) // CHUNK
        query_chunk = (qi * qb + lax.broadcasted_iota(jnp.int32, (kb, qb), 1)) // CHUNK
        s_ref[...] = jnp.where(key_chunk <= query_chunk, s_ref[...], -jnp.inf)
        return jnp.max(s_ref[...], axis=0, keepdims=True)

    def update(j, s_ref, mx, m, l):
        m_new = jnp.maximum(m, mx)
        alpha = jnp.exp2(m - m_new)
        p = jnp.exp2(s_ref[...] - m_new)
        l = alpha * l + jnp.sum(p, axis=0, keepdims=True)
        acc_ref[...] = alpha * acc_ref[...] + _dot(vt_ref[0, 0, j], p.astype(BF16))
        return m_new, l

    def pair(i, carry):
        mx0, m, l = carry
        j = blocks_per_q * i
        mx1 = raw_scores(j + 1, s1_ref)
        m, l = update(j, s0_ref, mx0, m, l)
        mx0 = raw_scores(j + 2, s0_ref)
        m, l = update(j + 1, s1_ref, mx1, m, l)
        return mx0, m, l

    acc_ref[...] = jnp.zeros(acc_ref.shape, F32)
    init = (raw_scores(0, s0_ref), jnp.full((1, qb), -jnp.inf, F32), jnp.zeros((1, qb), F32))
    _, m, l = lax.fori_loop(0, qi, pair, init)

    j = blocks_per_q * qi
    raw_scores(j + 1, s1_ref)
    m, l = update(j, s0_ref, mask_scores(j, s0_ref), m, l)
    m, l = update(j + 1, s1_ref, mask_scores(j + 1, s1_ref), m, l)
    o_ref[0] = (acc_ref[...] / l).T.astype(BF16)


def _attention(qt, k, vt):
    b, n_heads, s, _ = k.shape
    assert ATTN_Q_BLOCK == 2 * ATTN_BLOCK
    return pl.pallas_call(
        _attn_kernel,
        grid=(b, n_heads, s // ATTN_Q_BLOCK),
        in_specs=[pl.BlockSpec((1, 1, HEAD_PAD, ATTN_Q_BLOCK), lambda bi, hi, qi: (bi, hi, 0, qi)),
                  pl.BlockSpec((1, 1, s, HEAD_PAD), lambda bi, hi, qi: (bi, hi, 0, 0)),
                  pl.BlockSpec((1, 1, s // ATTN_BLOCK, V_HEAD, ATTN_BLOCK),
                               lambda bi, hi, qi: (bi, hi, 0, 0, 0))],
        out_specs=pl.BlockSpec((1, ATTN_Q_BLOCK, V_HEAD), lambda bi, hi, qi: (bi, qi, hi)),
        out_shape=jax.ShapeDtypeStruct((b, s, n_heads * V_HEAD), BF16),
        scratch_shapes=[pltpu.VMEM((ATTN_BLOCK, ATTN_Q_BLOCK), F32),
                        pltpu.VMEM((ATTN_BLOCK, ATTN_Q_BLOCK), F32),
                        pltpu.VMEM((V_HEAD, ATTN_Q_BLOCK), F32)],
        compiler_params=_params(3),
        name="mla_attention",
    )(qt, k, vt)


def _out_proj_kernel(h_ref, o_ref, w_ref, out_ref):
    out_ref[...] = h_ref[...] + _dot(o_ref[...], w_ref[...])


def _out_proj(h, o, w_o):
    t, d = h.shape
    row = lambda width: pl.BlockSpec((FFN_ROWS, width), lambda i: (i, 0))
    return pl.pallas_call(
        _out_proj_kernel,
        grid=(t // FFN_ROWS,),
        in_specs=[row(d), row(o.shape[1]), _resident(w_o.shape)],
        out_specs=row(d),
        out_shape=jax.ShapeDtypeStruct((t, d), F32),
        compiler_params=_params(1),
        name="attn_out_proj",
    )(h, o, w_o)


def kernel(x, positions, ffn_norm1, ffn1_w1, ffn1_w3, ffn1_w2, mix_norm, ffn_norm2, ffn2_w1, ffn2_w3, ffn2_w2, conv_w_pw1, conv_w_dw, conv_norm, conv_w_pw2, mla_w_a, mla_q_norm, mla_kv_norm, mla_w_uq, mla_w_ukv, mla_w_o, final_norm):
    b, s, d = x.shape
    depth = ffn_norm1.shape[0]
    n_mixers = 2
    bf = lambda w: w.astype(BF16)
    h = x
    for i in range(depth):
        h = _ffn(h.reshape(b * s, d), ffn_norm1[i], bf(ffn1_w1[i]), bf(ffn1_w3[i]),
                 bf(ffn1_w2[i])).reshape(b, s, d)
        j = i // n_mixers
        if i % n_mixers == 0:
            h = _conv_module(h, mix_norm[i], bf(conv_w_pw1[j]), conv_w_dw[j], conv_norm[j],
                             bf(conv_w_pw2[j]))
        else:
            q, k, v = _mla_proj(h, positions, mix_norm[i], mla_w_a[j], mla_q_norm[j],
                                mla_kv_norm[j], mla_w_uq[j], mla_w_ukv[j])
            o = _attention(q, k, v)
            h = _out_proj(h.reshape(b * s, d), o.reshape(b * s, d), bf(mla_w_o[j])).reshape(b, s, d)
        last = i == depth - 1
        h = _ffn(h.reshape(b * s, d), ffn_norm2[i], bf(ffn2_w1[i]), bf(ffn2_w3[i]),
                 bf(ffn2_w2[i]), final_gain=final_norm if last else None).reshape(b, s, d)
    return h
```

```python
import functools

import jax
import jax.numpy as jnp
from jax import lax
from jax.experimental import pallas as pl
from jax.experimental.pallas import tpu as pltpu

CHUNK = 64
CONV_WIDTH = 31
N_HEADS = 8
QK_NOPE = 128
QK_ROPE = 64
V_HEAD = 128
V_AUG = V_HEAD + 16
Q_LORA = 512
KV_LORA = 256
ROPE_THETA = 10000.0
RMS_EPS = 1e-6
FFN_RES_WEIGHT = 0.5
LOG2_E = 1.4426950408889634

LANES = 128
SUBLANES = 8
VMEM_LIMIT_BYTES = 56 * 1024 * 1024

FFN_ROWS = 512
CONV_ROWS = 512
CONV_TAIL = 32
CONV_ROW_BLOCK = 64
CONV_STAGE_ROWS = 128
PROJ_ROWS = 512
ATTN_BLOCK = 512
ATTN_Q_BLOCK = 2048
HEAD_PAD = 2 * LANES

BF16 = jnp.bfloat16
F32 = jnp.float32


def _rms(x, g):
    return x * lax.rsqrt(jnp.mean(x * x, axis=-1, keepdims=True) + RMS_EPS) * g


def _dot(a, b):
    return jnp.dot(a, b, preferred_element_type=F32)


def _resident(shape):
    zeros = (0,) * len(shape)
    return pl.BlockSpec(shape, lambda *_: zeros, pipeline_mode=pl.Buffered(1))


def _params(n_axes):
    return pltpu.CompilerParams(
        dimension_semantics=("arbitrary",) * n_axes,
        vmem_limit_bytes=VMEM_LIMIT_BYTES)


def _ffn_body(h, g, w1_ref, w3_ref, w2_ref):
    x = _rms(h, g).astype(BF16)
    gate = _dot(x, w1_ref[...])
    up = _dot(x, w3_ref[...])
    a = (gate * jax.nn.sigmoid(gate) * up).astype(BF16)
    return h + FFN_RES_WEIGHT * _dot(a, w2_ref[...])


def _ffn_kernel(*refs, with_attn, with_final):
    refs = list(refs)
    o_ref = refs.pop()
    h = refs.pop(0)[...]
    if with_attn:
        a_ref, wo_ref = refs.pop(0), refs.pop(0)
        h = h + _dot(a_ref[...], wo_ref[...])
    g_ref, w1_ref, w3_ref, w2_ref = refs[:4]
    out = _ffn_body(h, g_ref[...], w1_ref, w3_ref, w2_ref)
    if with_final:
        out = _rms(out, refs[4][...])
    o_ref[...] = out


def _ffn(h, g, w1, w3, w2, attn=None, w_o=None, final_gain=None):
    t, d = h.shape
    row = lambda width: pl.BlockSpec((FFN_ROWS, width), lambda i: (i, 0))
    in_specs, args = [row(d)], [h]
    if attn is not None:
        in_specs += [row(attn.shape[1]), _resident(w_o.shape)]
        args += [attn, w_o]
    in_specs += [_resident((1, d)), _resident(w1.shape), _resident(w3.shape), _resident(w2.shape)]
    args += [g.reshape(1, d), w1, w3, w2]
    if final_gain is not None:
        in_specs.append(_resident((1, d)))
        args.append(final_gain.reshape(1, d))
    return pl.pallas_call(
        functools.partial(_ffn_kernel, with_attn=attn is not None,
                          with_final=final_gain is not None),
        grid=(t // FFN_ROWS,),
        in_specs=in_specs,
        out_specs=row(d),
        out_shape=jax.ShapeDtypeStruct((t, d), F32),
        compiler_params=_params(1),
        name="ffn" + ("_attn" if attn is not None else "") + ("_final" if final_gain is not None else ""),
    )(*args)


def _conv_kernel(h_ref, g_ref, wpw1_ref, wdw_ref, gn_ref, wpw2_ref, o_ref, u_ref, y_ref):
    rows, d = y_ref.shape
    n_lane_tiles = d // LANES

    @pl.when(pl.program_id(1) == 0)
    def _():
        u_ref[:, 0:CONV_TAIL, :] = jnp.zeros((n_lane_tiles, CONV_TAIL, LANES), F32)

    first = CONV_TAIL - (CONV_WIDTH - 1)
    for stage in range(rows // CONV_STAGE_ROWS):
        r_lo = stage * CONV_STAGE_ROWS
        stage_rows = slice(r_lo, r_lo + CONV_STAGE_ROWS)
        h = h_ref[0, stage_rows, :]
        m = _rms(h, g_ref[...]).astype(BF16)
        u = _dot(m, wpw1_ref[...])
        glu = u[:, :d] * jax.nn.sigmoid(u[:, d:])
        for c in range(n_lane_tiles):
            u_ref[c, CONV_TAIL + r_lo:CONV_TAIL + r_lo + CONV_STAGE_ROWS, :] = (
                glu[:, c * LANES:(c + 1) * LANES])

        for c in range(n_lane_tiles):
            lanes = slice(c * LANES, (c + 1) * LANES)
            for r0 in range(r_lo, r_lo + CONV_STAGE_ROWS, CONV_ROW_BLOCK):
                acc = wdw_ref[0:1, lanes] * u_ref[c, r0 + first:r0 + first + CONV_ROW_BLOCK, :]
                for k in range(1, CONV_WIDTH):
                    lo = r0 + first + k
                    acc = acc + wdw_ref[k:k + 1, lanes] * u_ref[c, lo:lo + CONV_ROW_BLOCK, :]
                y_ref[r0:r0 + CONV_ROW_BLOCK, lanes] = acc

        z = _rms(y_ref[stage_rows, :], gn_ref[...])
        z = (z * jax.nn.sigmoid(z)).astype(BF16)
        o_ref[0, stage_rows, :] = h + _dot(z, wpw2_ref[...])

    u_ref[:, 0:CONV_TAIL, :] = u_ref[:, rows:rows + CONV_TAIL, :]


def _conv_module(h, g, w_pw1, w_dw, g_norm, w_pw2):
    b, s, d = h.shape
    row = pl.BlockSpec((1, CONV_ROWS, d), lambda bi, si: (bi, si, 0))
    return pl.pallas_call(
        _conv_kernel,
        grid=(b, s // CONV_ROWS),
        in_specs=[row, _resident((1, d)), _resident((d, 2 * d)), _resident((CONV_WIDTH, d)),
                  _resident((1, d)), _resident((d, d))],
        out_specs=row,
        out_shape=jax.ShapeDtypeStruct((b, s, d), F32),
        scratch_shapes=[pltpu.VMEM((d // LANES, CONV_TAIL + CONV_ROWS, LANES), F32),
                        pltpu.VMEM((CONV_ROWS, d), F32)],
        compiler_params=_params(2),
        name="conv_module",
    )(h, g.reshape(1, d), w_pw1, w_dw, g_norm.reshape(1, d), w_pw2)


def _rope_tables(ang, index):
    half = QK_ROPE // 2
    in_rope = index < QK_ROPE
    cos_t = jnp.where(in_rope, jnp.cos(ang), 0.0)
    sin_t = jnp.where(index < half, -jnp.sin(ang), jnp.where(in_rope, jnp.sin(ang), 0.0))
    return cos_t, sin_t


def _dot_nt(a, b):
    return lax.dot_general(a, b, (((1,), (1,)), ((), ())), preferred_element_type=F32)


def _mla_proj_kernel(h_ref, pos_col_ref, pos_row_ref, freq_row_ref, freq_col_ref, g_ref, wa_ref,
                     gq_ref, gkv_ref, wqn_ref, wqr_ref, wqs_ref, wk_ref, wv_ref,
                     qt_ref, k_ref, vt_ref):
    m = _rms(h_ref[0], g_ref[...]).astype(BF16)
    a = _dot(m, wa_ref[...])
    c_q = _rms(a[:, :Q_LORA], gq_ref[...]).astype(BF16)
    c_kv = _rms(a[:, Q_LORA:Q_LORA + KV_LORA], gkv_ref[...]).astype(BF16)
    kr = a[:, Q_LORA + KV_LORA:Q_LORA + KV_LORA + LANES]
    kr_swapped = a[:, Q_LORA + KV_LORA + LANES:]

    cos_k, sin_k = _rope_tables(pos_col_ref[0] * freq_row_ref[...],
                                lax.broadcasted_iota(jnp.int32, (1, LANES), 1))
    cos_q, sin_q = _rope_tables(freq_col_ref[...] * pos_row_ref[0],
                                lax.broadcasted_iota(jnp.int32, (LANES, 1), 0))

    scale = (QK_NOPE + QK_ROPE) ** -0.5 * LOG2_E
    qt_nope = _dot_nt(wqn_ref[...], c_q)
    qt_rope = _dot_nt(wqr_ref[...], c_q)
    qt_rope_swapped = _dot_nt(wqs_ref[...], c_q)
    k_nope = _dot(c_kv, wk_ref[...])
    vt = _dot_nt(wv_ref[...], c_kv)
    k_rot = (kr * cos_k + kr_swapped * sin_k).astype(BF16)
    aug_row = lax.broadcasted_iota(jnp.int32, (V_AUG - V_HEAD, vt.shape[1]), 0)
    sum_rows = jnp.where(aug_row == 0, 1.0, 0.0).astype(BF16)
    for hd in range(N_HEADS):
        sl = slice(hd * LANES, (hd + 1) * LANES)
        qt_rot = qt_rope[sl, :] * cos_q + qt_rope_swapped[sl, :] * sin_q
        qt_ref[0, hd, 0:LANES, :] = (qt_nope[sl, :] * scale).astype(BF16)
        qt_ref[0, hd, LANES:HEAD_PAD, :] = (qt_rot * scale).astype(BF16)
        k_ref[0, hd, :, 0:LANES] = k_nope[:, sl].astype(BF16)
        k_ref[0, hd, :, LANES:HEAD_PAD] = k_rot
        vt_ref[0, hd, 0, 0:V_HEAD, :] = vt[sl, :].astype(BF16)
        vt_ref[0, hd, 0, V_HEAD:V_AUG, :] = sum_rows


def _rope_tile_weights(w_rope):
    c, g, _ = w_rope.shape
    half = QK_ROPE // 2
    x1, x2 = w_rope[..., :half], w_rope[..., half:]
    pad = jnp.zeros((c, g, LANES - QK_ROPE), w_rope.dtype)
    plain = jnp.concatenate([x1, x2, pad], axis=-1).reshape(c, g * LANES)
    swapped = jnp.concatenate([x2, x1, pad], axis=-1).reshape(c, g * LANES)
    return plain, swapped


def _mla_proj(h, pos, g, w_a, g_q, g_kv, w_uq, w_ukv):
    b, s, d = h.shape
    kr_plain, kr_swapped = _rope_tile_weights(w_a[:, None, Q_LORA + KV_LORA:])
    wa = jnp.concatenate([w_a[:, :Q_LORA + KV_LORA], kr_plain, kr_swapped], axis=1).astype(BF16)
    wqn_t = w_uq[:, :, :QK_NOPE].reshape(Q_LORA, N_HEADS * QK_NOPE).T.astype(BF16)
    wqr, wqs = _rope_tile_weights(w_uq[:, :, QK_NOPE:])
    wqr_t, wqs_t = wqr.T.astype(BF16), wqs.T.astype(BF16)
    wk = w_ukv[:, :, :QK_NOPE].reshape(KV_LORA, N_HEADS * QK_NOPE).astype(BF16)
    wv_t = w_ukv[:, :, QK_NOPE:].reshape(KV_LORA, N_HEADS * V_HEAD).T.astype(BF16)
    inv_freq = ROPE_THETA ** (-2.0 * jnp.arange(QK_ROPE // 2, dtype=F32) / QK_ROPE)
    freq_tile = jnp.tile(inv_freq, LANES // (QK_ROPE // 2))
    pos_f = pos.astype(F32)

    assert PROJ_ROWS == ATTN_BLOCK
    row = pl.BlockSpec((1, PROJ_ROWS, d), lambda bi, si: (bi, si, 0))
    return pl.pallas_call(
        _mla_proj_kernel,
        grid=(b, s // PROJ_ROWS),
        in_specs=[row,
                  pl.BlockSpec((1, PROJ_ROWS, 1), lambda bi, si: (bi, si, 0)),
                  pl.BlockSpec((1, 1, PROJ_ROWS), lambda bi, si: (bi, 0, si)),
                  _resident((1, LANES)), _resident((LANES, 1)), _resident((1, d)),
                  _resident(wa.shape), _resident((1, Q_LORA)), _resident((1, KV_LORA)),
                  _resident(wqn_t.shape), _resident(wqr_t.shape), _resident(wqs_t.shape),
                  _resident(wk.shape), _resident(wv_t.shape)],
        out_specs=[pl.BlockSpec((1, N_HEADS, HEAD_PAD, PROJ_ROWS), lambda bi, si: (bi, 0, 0, si)),
                   pl.BlockSpec((1, N_HEADS, PROJ_ROWS, HEAD_PAD), lambda bi, si: (bi, 0, si, 0)),
                   pl.BlockSpec((1, N_HEADS, 1, V_AUG, PROJ_ROWS),
                                lambda bi, si: (bi, 0, si, 0, 0))],
        out_shape=[jax.ShapeDtypeStruct((b, N_HEADS, HEAD_PAD, s), BF16),
                   jax.ShapeDtypeStruct((b, N_HEADS, s, HEAD_PAD), BF16),
                   jax.ShapeDtypeStruct((b, N_HEADS, s // PROJ_ROWS, V_AUG, PROJ_ROWS), BF16)],
        compiler_params=_params(2),
        name="mla_proj",
    )(h, pos_f[:, :, None], pos_f[:, None, :], freq_tile.reshape(1, LANES),
      freq_tile.reshape(LANES, 1), g.reshape(1, d), wa, g_q.reshape(1, Q_LORA),
      g_kv.reshape(1, KV_LORA), wqn_t, wqr_t, wqs_t, wk, wv_t)


def _sublane_all(x, op):
    for shift in (4, 2, 1):
        x = op(x, pltpu.roll(x, shift, axis=0))
    return x


def _attn_kernel(qt_ref, k_ref, vt_ref, bias_ref, o_ref, s0_ref, s1_ref, mx0_ref, mx1_ref,
                 m_ref, acc_ref):
    qb, kb = ATTN_Q_BLOCK, ATTN_BLOCK
    n_sub = qb // kb
    qi = pl.program_id(2)
    bufs = ((s0_ref, mx0_ref), (s1_ref, mx1_ref))

    def keys(j):
        return k_ref[0, 0, pl.ds(pl.multiple_of(j * kb, kb), kb), :]

    def slabs(x):
        return x.reshape(x.shape[0] // SUBLANES, SUBLANES, x.shape[1])

    def raw_scores(j, buf):
        s_ref, mx_ref = buf
        s_ref[...] = _dot(keys(j), qt_ref[0, 0])
        mx_ref[...] = jnp.max(slabs(s_ref[...]), axis=0)

    def diagonal_scores(j, buf, lo):
        s_ref, mx_ref = buf
        k = keys(j)
        s_ref[:, lo:lo + kb] = _dot(k, qt_ref[0, 0, :, lo:lo + kb]) + bias_ref[...]
        if lo + kb < qb:
            s_ref[:, lo + kb:] = _dot(k, qt_ref[0, 0, :, lo + kb:])
        mx_ref[:, lo:] = jnp.max(slabs(s_ref[:, lo:]), axis=0)

    def update(j, buf, lo=0):
        s_ref, mx_ref = buf
        m_old = m_ref[:, lo:]
        m_new = jnp.maximum(m_old, _sublane_all(mx_ref[:, lo:], jnp.maximum))
        alpha = jnp.exp2(m_old - m_new)
        p = jnp.exp2(slabs(s_ref[:, lo:]) - m_new[None])
        m_ref[:, lo:] = m_new
        pv = _dot(vt_ref[0, 0, j], p.reshape(kb, qb - lo).astype(BF16))
        acc_ref[:, lo:] = (slabs(acc_ref[:, lo:]) * alpha[None] + slabs(pv)).reshape(pv.shape)

    def trip(i, carry):
        j = n_sub * i
        for t in range(n_sub):
            raw_scores(j + t + 1, bufs[(t + 1) % 2])
            update(j + t, bufs[t % 2])
        return carry

    acc_ref[...] = jnp.zeros(acc_ref.shape, F32)
    m_ref[...] = jnp.full(m_ref.shape, -jnp.inf, F32)
    raw_scores(0, bufs[0])
    lax.fori_loop(0, qi, trip, 0)

    j = n_sub * qi
    s0_ref[:, 0:kb] = s0_ref[:, 0:kb] + bias_ref[...]
    mx0_ref[:, 0:kb] = jnp.max(slabs(s0_ref[:, 0:kb]), axis=0)
    for t in range(n_sub):
        if t + 1 < n_sub:
            diagonal_scores(j + t + 1, bufs[(t + 1) % 2], (t + 1) * kb)
        update(j + t, bufs[t % 2], lo=t * kb)
    o_ref[0] = (acc_ref[0:V_HEAD, :] / acc_ref[V_HEAD:V_HEAD + 1, :]).T.astype(BF16)


def _attention(qt, k, vt):
    b, n_heads, s, _ = k.shape
    assert ATTN_Q_BLOCK % (2 * ATTN_BLOCK) == 0 and ATTN_BLOCK % CHUNK == 0
    chunk_of = jnp.arange(ATTN_BLOCK, dtype=jnp.int32) // CHUNK
    bias = jnp.where(chunk_of[:, None] <= chunk_of[None, :], 0.0, -jnp.inf).astype(F32)
    return pl.pallas_call(
        _attn_kernel,
        grid=(b, n_heads, s // ATTN_Q_BLOCK),
        in_specs=[pl.BlockSpec((1, 1, HEAD_PAD, ATTN_Q_BLOCK), lambda bi, hi, qi: (bi, hi, 0, qi)),
                  pl.BlockSpec((1, 1, s, HEAD_PAD), lambda bi, hi, qi: (bi, hi, 0, 0)),
                  pl.BlockSpec((1, 1, s // ATTN_BLOCK, V_AUG, ATTN_BLOCK),
                               lambda bi, hi, qi: (bi, hi, 0, 0, 0)),
                  _resident((ATTN_BLOCK, ATTN_BLOCK))],
        out_specs=pl.BlockSpec((1, ATTN_Q_BLOCK, V_HEAD), lambda bi, hi, qi: (bi, qi, hi)),
        out_shape=jax.ShapeDtypeStruct((b, s, n_heads * V_HEAD), BF16),
        scratch_shapes=[pltpu.VMEM((ATTN_BLOCK, ATTN_Q_BLOCK), F32),
                        pltpu.VMEM((ATTN_BLOCK, ATTN_Q_BLOCK), F32),
                        pltpu.VMEM((SUBLANES, ATTN_Q_BLOCK), F32),
                        pltpu.VMEM((SUBLANES, ATTN_Q_BLOCK), F32),
                        pltpu.VMEM((SUBLANES, ATTN_Q_BLOCK), F32),
                        pltpu.VMEM((V_AUG, ATTN_Q_BLOCK), F32)],
        compiler_params=_params(3),
        name="mla_attention",
    )(qt, k, vt, bias)


def kernel(x, positions, ffn_norm1, ffn1_w1, ffn1_w3, ffn1_w2, mix_norm, ffn_norm2, ffn2_w1, ffn2_w3, ffn2_w2, conv_w_pw1, conv_w_dw, conv_norm, conv_w_pw2, mla_w_a, mla_q_norm, mla_kv_norm, mla_w_uq, mla_w_ukv, mla_w_o, final_norm):
    b, s, d = x.shape
    depth = ffn_norm1.shape[0]
    n_mixers = 2
    bf = lambda w: w.astype(BF16)
    h = x
    for i in range(depth):
        h = _ffn(h.reshape(b * s, d), ffn_norm1[i], bf(ffn1_w1[i]), bf(ffn1_w3[i]),
                 bf(ffn1_w2[i])).reshape(b, s, d)
        j = i // n_mixers
        attn, w_o = None, None
        if i % n_mixers == 0:
            h = _conv_module(h, mix_norm[i], bf(conv_w_pw1[j]), conv_w_dw[j], conv_norm[j],
                             bf(conv_w_pw2[j]))
        else:
            qt, k, vt = _mla_proj(h, positions, mix_norm[i], mla_w_a[j], mla_q_norm[j],
                                  mla_kv_norm[j], mla_w_uq[j], mla_w_ukv[j])
            attn, w_o = _attention(qt, k, vt).reshape(b * s, -1), bf(mla_w_o[j])
        last = i == depth - 1
        h = _ffn(h.reshape(b * s, d), ffn_norm2[i], bf(ffn2_w1[i]), bf(ffn2_w3[i]),
                 bf(ffn2_w2[i]), attn=attn, w_o=w_o,
                 final_gain=final_norm if last else None).reshape(b, s, d)
    return h
```

```python
import functools

import jax
import jax.numpy as jnp
from jax import lax
from jax.experimental import pallas as pl
from jax.experimental.pallas import tpu as pltpu

CHUNK = 64
CONV_WIDTH = 31
N_HEADS = 8
QK_NOPE = 128
QK_ROPE = 64
V_HEAD = 128
V_AUG = V_HEAD + 16
Q_LORA = 512
KV_LORA = 256
ROPE_THETA = 10000.0
RMS_EPS = 1e-6
FFN_RES_WEIGHT = 0.5
LOG2_E = 1.4426950408889634

LANES = 128
SUBLANES = 8
VMEM_LIMIT_BYTES = 56 * 1024 * 1024

FFN_ROWS = 512
CONV_ROWS = 512
CONV_TAIL = 32
CONV_ROW_BLOCK = 64
CONV_STAGE_ROWS = 256
PROJ_ROWS = 512
ATTN_BLOCK = 512
ATTN_Q_BLOCK = 2048
ATTN_COL_CHUNK = 256
HEAD_PAD = 2 * LANES

BF16 = jnp.bfloat16
F32 = jnp.float32


def _rms(x, g):
    return x * lax.rsqrt(jnp.mean(x * x, axis=-1, keepdims=True) + RMS_EPS) * g


def _dot(a, b):
    return jnp.dot(a, b, preferred_element_type=F32)


def _resident(shape):
    zeros = (0,) * len(shape)
    return pl.BlockSpec(shape, lambda *_: zeros, pipeline_mode=pl.Buffered(1))


def _params(n_axes):
    return pltpu.CompilerParams(
        dimension_semantics=("arbitrary",) * n_axes,
        vmem_limit_bytes=VMEM_LIMIT_BYTES)


def _ffn_body(h, g, w1_ref, w3_ref, w2_ref):
    x = _rms(h, g).astype(BF16)
    gate = _dot(x, w1_ref[...])
    up = _dot(x, w3_ref[...])
    a = (gate * jax.nn.sigmoid(gate) * up).astype(BF16)
    return h + FFN_RES_WEIGHT * _dot(a, w2_ref[...])


def _ffn_kernel(*refs, with_attn, with_final):
    refs = list(refs)
    o_ref = refs.pop()
    h = refs.pop(0)[...]
    if with_attn:
        a_ref, wo_ref = refs.pop(0), refs.pop(0)
        h = h + _dot(a_ref[...], wo_ref[...])
    g_ref, w1_ref, w3_ref, w2_ref = refs[:4]
    out = _ffn_body(h, g_ref[...], w1_ref, w3_ref, w2_ref)
    if with_final:
        out = _rms(out, refs[4][...])
    o_ref[...] = out


def _ffn(h, g, w1, w3, w2, attn=None, w_o=None, final_gain=None):
    t, d = h.shape
    row = lambda width: pl.BlockSpec((FFN_ROWS, width), lambda i: (i, 0))
    in_specs, args = [row(d)], [h]
    if attn is not None:
        in_specs += [row(attn.shape[1]), _resident(w_o.shape)]
        args += [attn, w_o]
    in_specs += [_resident((1, d)), _resident(w1.shape), _resident(w3.shape), _resident(w2.shape)]
    args += [g.reshape(1, d), w1, w3, w2]
    if final_gain is not None:
        in_specs.append(_resident((1, d)))
        args.append(final_gain.reshape(1, d))
    return pl.pallas_call(
        functools.partial(_ffn_kernel, with_attn=attn is not None,
                          with_final=final_gain is not None),
        grid=(t // FFN_ROWS,),
        in_specs=in_specs,
        out_specs=row(d),
        out_shape=jax.ShapeDtypeStruct((t, d), F32),
        compiler_params=_params(1),
        name="ffn" + ("_attn" if attn is not None else "") + ("_final" if final_gain is not None else ""),
    )(*args)


def _conv_kernel(h_ref, g_ref, wpw1_ref, wdw_ref, gn_ref, wpw2_ref, o_ref, u_ref, y_ref):
    rows, d = y_ref.shape
    n_lane_tiles = d // LANES

    @pl.when(pl.program_id(1) == 0)
    def _():
        u_ref[:, 0:CONV_TAIL, :] = jnp.zeros((n_lane_tiles, CONV_TAIL, LANES), F32)

    first = CONV_TAIL - (CONV_WIDTH - 1)
    for stage in range(rows // CONV_STAGE_ROWS):
        r_lo = stage * CONV_STAGE_ROWS
        stage_rows = slice(r_lo, r_lo + CONV_STAGE_ROWS)
        h = h_ref[0, stage_rows, :]
        m = _rms(h, g_ref[...]).astype(BF16)
        u = _dot(m, wpw1_ref[...])
        glu = u[:, :d] * jax.nn.sigmoid(u[:, d:])
        for c in range(n_lane_tiles):
            u_ref[c, CONV_TAIL + r_lo:CONV_TAIL + r_lo + CONV_STAGE_ROWS, :] = (
                glu[:, c * LANES:(c + 1) * LANES])

        for c in range(n_lane_tiles):
            lanes = slice(c * LANES, (c + 1) * LANES)
            for r0 in range(r_lo, r_lo + CONV_STAGE_ROWS, CONV_ROW_BLOCK):
                acc = wdw_ref[0:1, lanes] * u_ref[c, r0 + first:r0 + first + CONV_ROW_BLOCK, :]
                for k in range(1, CONV_WIDTH):
                    lo = r0 + first + k
                    acc = acc + wdw_ref[k:k + 1, lanes] * u_ref[c, lo:lo + CONV_ROW_BLOCK, :]
                y_ref[r0:r0 + CONV_ROW_BLOCK, lanes] = acc

        z = _rms(y_ref[stage_rows, :], gn_ref[...])
        z = (z * jax.nn.sigmoid(z)).astype(BF16)
        o_ref[0, stage_rows, :] = h + _dot(z, wpw2_ref[...])

    u_ref[:, 0:CONV_TAIL, :] = u_ref[:, rows:rows + CONV_TAIL, :]


def _conv_module(h, g, w_pw1, w_dw, g_norm, w_pw2):
    b, s, d = h.shape
    row = pl.BlockSpec((1, CONV_ROWS, d), lambda bi, si: (bi, si, 0))
    return pl.pallas_call(
        _conv_kernel,
        grid=(b, s // CONV_ROWS),
        in_specs=[row, _resident((1, d)), _resident((d, 2 * d)), _resident((CONV_WIDTH, d)),
                  _resident((1, d)), _resident((d, d))],
        out_specs=row,
        out_shape=jax.ShapeDtypeStruct((b, s, d), F32),
        scratch_shapes=[pltpu.VMEM((d // LANES, CONV_TAIL + CONV_ROWS, LANES), F32),
                        pltpu.VMEM((CONV_ROWS, d), F32)],
        compiler_params=_params(2),
        name="conv_module",
    )(h, g.reshape(1, d), w_pw1, w_dw, g_norm.reshape(1, d), w_pw2)


def _dot_nt(a, b):
    return lax.dot_general(a, b, (((1,), (1,)), ((), ())), preferred_element_type=F32)


def _mla_proj_kernel(h_ref, pos_ref, freq_ref, g_ref, wa_ref,
                     gq_ref, gkv_ref, wqn_ref, wqr_ref, wqs_ref, wk_ref, wv_ref,
                     qt_ref, k_ref, vt_ref):
    m = _rms(h_ref[0], g_ref[...]).astype(BF16)
    a = _dot(m, wa_ref[...])
    c_q = _rms(a[:, :Q_LORA], gq_ref[...]).astype(BF16)
    c_kv = _rms(a[:, Q_LORA:Q_LORA + KV_LORA], gkv_ref[...]).astype(BF16)
    kr = a[:, Q_LORA + KV_LORA:Q_LORA + KV_LORA + LANES]
    kr_swapped = a[:, Q_LORA + KV_LORA + LANES:]

    ang = freq_ref[...] * pos_ref[0]
    cos, sin = jnp.cos(ang), jnp.sin(ang)
    pad = jnp.zeros((LANES - QK_ROPE, ang.shape[1]), F32)
    cos_q = jnp.concatenate([cos, cos, pad], axis=0)
    sin_q = jnp.concatenate([-sin, sin, pad], axis=0)
    cos_k, sin_k = cos_q.T, sin_q.T

    scale = (QK_NOPE + QK_ROPE) ** -0.5 * LOG2_E
    qt_nope = _dot_nt(wqn_ref[...], c_q)
    qt_rope = _dot_nt(wqr_ref[...], c_q)
    qt_rope_swapped = _dot_nt(wqs_ref[...], c_q)
    k_nope = _dot(c_kv, wk_ref[...])
    vt = _dot_nt(wv_ref[...], c_kv)
    k_rot = (kr * cos_k + kr_swapped * sin_k).astype(BF16)
    aug_row = lax.broadcasted_iota(jnp.int32, (V_AUG - V_HEAD, vt.shape[1]), 0)
    sum_rows = jnp.where(aug_row == 0, 1.0, 0.0).astype(BF16)
    for hd in range(N_HEADS):
        sl = slice(hd * LANES, (hd + 1) * LANES)
        qt_rot = qt_rope[sl, :] * cos_q + qt_rope_swapped[sl, :] * sin_q
        qt_ref[0, hd, 0:LANES, :] = (qt_nope[sl, :] * scale).astype(BF16)
        qt_ref[0, hd, LANES:HEAD_PAD, :] = (qt_rot * scale).astype(BF16)
        k_ref[0, hd, :, 0:LANES] = k_nope[:, sl].astype(BF16)
        k_ref[0, hd, :, LANES:HEAD_PAD] = k_rot
        vt_ref[0, hd, 0, 0:V_HEAD, :] = vt[sl, :].astype(BF16)
        vt_ref[0, hd, 0, V_HEAD:V_AUG, :] = sum_rows


def _rope_tile_weights(w_rope):
    c, g, _ = w_rope.shape
    half = QK_ROPE // 2
    x1, x2 = w_rope[..., :half], w_rope[..., half:]
    pad = jnp.zeros((c, g, LANES - QK_ROPE), w_rope.dtype)
    plain = jnp.concatenate([x1, x2, pad], axis=-1).reshape(c, g * LANES)
    swapped = jnp.concatenate([x2, x1, pad], axis=-1).reshape(c, g * LANES)
    return plain, swapped


def _mla_proj(h, pos, g, w_a, g_q, g_kv, w_uq, w_ukv):
    b, s, d = h.shape
    kr_plain, kr_swapped = _rope_tile_weights(w_a[:, None, Q_LORA + KV_LORA:])
    wa = jnp.concatenate([w_a[:, :Q_LORA + KV_LORA], kr_plain, kr_swapped], axis=1).astype(BF16)
    wqn_t = w_uq[:, :, :QK_NOPE].reshape(Q_LORA, N_HEADS * QK_NOPE).T.astype(BF16)
    wqr, wqs = _rope_tile_weights(w_uq[:, :, QK_NOPE:])
    wqr_t, wqs_t = wqr.T.astype(BF16), wqs.T.astype(BF16)
    wk = w_ukv[:, :, :QK_NOPE].reshape(KV_LORA, N_HEADS * QK_NOPE).astype(BF16)
    wv_t = w_ukv[:, :, QK_NOPE:].reshape(KV_LORA, N_HEADS * V_HEAD).T.astype(BF16)
    inv_freq = ROPE_THETA ** (-2.0 * jnp.arange(QK_ROPE // 2, dtype=F32) / QK_ROPE)
    pos_f = pos.astype(F32)

    assert PROJ_ROWS == ATTN_BLOCK
    row = pl.BlockSpec((1, PROJ_ROWS, d), lambda bi, si: (bi, si, 0))
    return pl.pallas_call(
        _mla_proj_kernel,
        grid=(b, s // PROJ_ROWS),
        in_specs=[row,
                  pl.BlockSpec((1, 1, PROJ_ROWS), lambda bi, si: (bi, 0, si)),
                  _resident((QK_ROPE // 2, 1)), _resident((1, d)),
                  _resident(wa.shape), _resident((1, Q_LORA)), _resident((1, KV_LORA)),
                  _resident(wqn_t.shape), _resident(wqr_t.shape), _resident(wqs_t.shape),
                  _resident(wk.shape), _resident(wv_t.shape)],
        out_specs=[pl.BlockSpec((1, N_HEADS, HEAD_PAD, PROJ_ROWS), lambda bi, si: (bi, 0, 0, si)),
                   pl.BlockSpec((1, N_HEADS, PROJ_ROWS, HEAD_PAD), lambda bi, si: (bi, 0, si, 0)),
                   pl.BlockSpec((1, N_HEADS, 1, V_AUG, PROJ_ROWS),
                                lambda bi, si: (bi, 0, si, 0, 0))],
        out_shape=[jax.ShapeDtypeStruct((b, N_HEADS, HEAD_PAD, s), BF16),
                   jax.ShapeDtypeStruct((b, N_HEADS, s, HEAD_PAD), BF16),
                   jax.ShapeDtypeStruct((b, N_HEADS, s // PROJ_ROWS, V_AUG, PROJ_ROWS), BF16)],
        compiler_params=_params(2),
        name="mla_proj",
    )(h, pos_f[:, None, :], inv_freq.reshape(QK_ROPE // 2, 1), g.reshape(1, d), wa,
      g_q.reshape(1, Q_LORA),
      g_kv.reshape(1, KV_LORA), wqn_t, wqr_t, wqs_t, wk, wv_t)


def _sublane_all(x, op):
    for shift in (4, 2, 1):
        x = op(x, pltpu.roll(x, shift, axis=0))
    return x


def _attn_kernel(qt_ref, k_ref, vt_ref, bias_ref, o_ref, s0_ref, s1_ref, mx0_ref, mx1_ref,
                 m_ref, acc_ref):
    qb, kb = ATTN_Q_BLOCK, ATTN_BLOCK
    n_sub = qb // kb
    qi = pl.program_id(2)
    bufs = ((s0_ref, mx0_ref), (s1_ref, mx1_ref))

    def keys(j):
        return k_ref[0, 0, pl.ds(pl.multiple_of(j * kb, kb), kb), :]

    def slabs(x):
        return x.reshape(x.shape[0] // SUBLANES, SUBLANES, x.shape[1])

    def raw_scores(j, buf, lo=0, hi=qb, masked=False):
        s_ref, mx_ref = buf
        scores = _dot(keys(j), qt_ref[0, 0, :, lo:hi])
        s_ref[:, lo:hi] = scores + bias_ref[...] if masked else scores
        mx_ref[:, lo:hi] = jnp.max(slabs(s_ref[:, lo:hi]), axis=0)

    def update(j, buf, lo=0, hi=qb):
        s_ref, mx_ref = buf
        m_old = m_ref[:, lo:hi]
        m_new = jnp.maximum(m_old, _sublane_all(mx_ref[:, lo:hi], jnp.maximum))
        alpha = jnp.exp2(m_old - m_new)
        p = jnp.exp2(slabs(s_ref[:, lo:hi]) - m_new[None])
        m_ref[:, lo:hi] = m_new
        pv = _dot(vt_ref[0, 0, j], p.reshape(kb, hi - lo).astype(BF16))
        acc_ref[:, lo:hi] = (slabs(acc_ref[:, lo:hi]) * alpha[None] + slabs(pv)).reshape(pv.shape)

    def trip(i, carry):
        j = n_sub * i
        for t in range(n_sub):
            for lo in range(0, qb, ATTN_COL_CHUNK):
                raw_scores(j + t + 1, bufs[(t + 1) % 2], lo, lo + ATTN_COL_CHUNK)
                update(j + t, bufs[t % 2], lo, lo + ATTN_COL_CHUNK)
        return carry

    acc_ref[...] = jnp.zeros(acc_ref.shape, F32)
    m_ref[...] = jnp.full(m_ref.shape, -jnp.inf, F32)
    raw_scores(0, bufs[0])
    lax.fori_loop(0, qi, trip, 0)

    j = n_sub * qi
    s0_ref[:, 0:kb] = s0_ref[:, 0:kb] + bias_ref[...]
    mx0_ref[:, 0:kb] = jnp.max(slabs(s0_ref[:, 0:kb]), axis=0)
    for t in range(n_sub):
        for c in range(t, n_sub):
            if t + 1 <= c:
                raw_scores(j + t + 1, bufs[(t + 1) % 2], c * kb, (c + 1) * kb, masked=c == t + 1)
            update(j + t, bufs[t % 2], c * kb, (c + 1) * kb)
    o_ref[0] = (acc_ref[0:V_HEAD, :] / acc_ref[V_HEAD:V_HEAD + 1, :]).T.astype(BF16)


def _attention(qt, k, vt):
    b, n_heads, s, _ = k.shape
    assert ATTN_Q_BLOCK % (2 * ATTN_BLOCK) == 0 and ATTN_BLOCK % CHUNK == 0
    chunk_of = jnp.arange(ATTN_BLOCK, dtype=jnp.int32) // CHUNK
    bias = jnp.where(chunk_of[:, None] <= chunk_of[None, :], 0.0, -jnp.inf).astype(F32)
    return pl.pallas_call(
        _attn_kernel,
        grid=(b, n_heads, s // ATTN_Q_BLOCK),
        in_specs=[pl.BlockSpec((1, 1, HEAD_PAD, ATTN_Q_BLOCK), lambda bi, hi, qi: (bi, hi, 0, qi)),
                  pl.BlockSpec((1, 1, s, HEAD_PAD), lambda bi, hi, qi: (bi, hi, 0, 0)),
                  pl.BlockSpec((1, 1, s // ATTN_BLOCK, V_AUG, ATTN_BLOCK),
                               lambda bi, hi, qi: (bi, hi, 0, 0, 0)),
                  _resident((ATTN_BLOCK, ATTN_BLOCK))],
        out_specs=pl.BlockSpec((1, ATTN_Q_BLOCK, V_HEAD), lambda bi, hi, qi: (bi, qi, hi)),
        out_shape=jax.ShapeDtypeStruct((b, s, n_heads * V_HEAD), BF16),
        scratch_shapes=[pltpu.VMEM((ATTN_BLOCK, ATTN_Q_BLOCK), F32),
                        pltpu.VMEM((ATTN_BLOCK, ATTN_Q_BLOCK), F32),
                        pltpu.VMEM((SUBLANES, ATTN_Q_BLOCK), F32),
                        pltpu.VMEM((SUBLANES, ATTN_Q_BLOCK), F32),
                        pltpu.VMEM((SUBLANES, ATTN_Q_BLOCK), F32),
                        pltpu.VMEM((V_AUG, ATTN_Q_BLOCK), F32)],
        compiler_params=_params(3),
        name="mla_attention",
    )(qt, k, vt, bias)


def kernel(x, positions, ffn_norm1, ffn1_w1, ffn1_w3, ffn1_w2, mix_norm, ffn_norm2, ffn2_w1, ffn2_w3, ffn2_w2, conv_w_pw1, conv_w_dw, conv_norm, conv_w_pw2, mla_w_a, mla_q_norm, mla_kv_norm, mla_w_uq, mla_w_ukv, mla_w_o, final_norm):
    b, s, d = x.shape
    depth = ffn_norm1.shape[0]
    n_mixers = 2
    bf = lambda w: w.astype(BF16)
    ffn1_w1, ffn1_w3, ffn1_w2 = bf(ffn1_w1), bf(ffn1_w3), bf(ffn1_w2)
    ffn2_w1, ffn2_w3, ffn2_w2 = bf(ffn2_w1), bf(ffn2_w3), bf(ffn2_w2)
    h = x
    for i in range(depth):
        h = _ffn(h.reshape(b * s, d), ffn_norm1[i], ffn1_w1[i], ffn1_w3[i],
                 ffn1_w2[i]).reshape(b, s, d)
        j = i // n_mixers
        attn, w_o = None, None
        if i % n_mixers == 0:
            h = _conv_module(h, mix_norm[i], bf(conv_w_pw1[j]), conv_w_dw[j], conv_norm[j],
                             bf(conv_w_pw2[j]))
        else:
            qt, k, vt = _mla_proj(h, positions, mix_norm[i], mla_w_a[j], mla_q_norm[j],
                                  mla_kv_norm[j], mla_w_uq[j], mla_w_ukv[j])
            attn, w_o = _attention(qt, k, vt).reshape(b * s, -1), bf(mla_w_o[j])
        last = i == depth - 1
        h = _ffn(h.reshape(b * s, d), ffn_norm2[i], ffn2_w1[i], ffn2_w3[i],
                 ffn2_w2[i], attn=attn, w_o=w_o,
                 final_gain=final_norm if last else None).reshape(b, s, d)
    return h
```

```python
import functools

import jax
import jax.numpy as jnp
from jax import lax
from jax.experimental import pallas as pl
from jax.experimental.pallas import tpu as pltpu

CHUNK = 64
CONV_WIDTH = 31
N_HEADS = 8
QK_NOPE = 128
QK_ROPE = 64
V_HEAD = 128
V_AUG = V_HEAD + 16
Q_LORA = 512
KV_LORA = 256
ROPE_THETA = 10000.0
RMS_EPS = 1e-6
FFN_RES_WEIGHT = 0.5
LOG2_E = 1.4426950408889634

LANES = 128
SUBLANES = 8
VMEM_LIMIT_BYTES = 56 * 1024 * 1024

FFN_ROWS = 512
CONV_ROWS = 512
CONV_TAIL = 32
CONV_ROW_BLOCK = 128
CONV_STAGE_ROWS = 256
PROJ_ROWS = 512
ATTN_BLOCK = 512
ATTN_Q_BLOCK = 2048
ATTN_COL_CHUNK = 256
HEAD_PAD = 2 * LANES

BF16 = jnp.bfloat16
F32 = jnp.float32


def _rms(x, g):
    return x * lax.rsqrt(jnp.mean(x * x, axis=-1, keepdims=True) + RMS_EPS) * g


def _dot(a, b):
    return jnp.dot(a, b, preferred_element_type=F32)


def _resident(shape):
    zeros = (0,) * len(shape)
    return pl.BlockSpec(shape, lambda *_: zeros, pipeline_mode=pl.Buffered(1))


def _params(n_axes):
    return pltpu.CompilerParams(
        dimension_semantics=("arbitrary",) * n_axes,
        vmem_limit_bytes=VMEM_LIMIT_BYTES)


def _ffn_body(h, g, w1_ref, w3_ref, w2_ref):
    x = _rms(h, g).astype(BF16)
    gate = _dot(x, w1_ref[...])
    up = _dot(x, w3_ref[...])
    a = (gate * jax.nn.sigmoid(gate) * up).astype(BF16)
    return h + FFN_RES_WEIGHT * _dot(a, w2_ref[...])


def _ffn_kernel(*refs, with_attn, with_final):
    refs = list(refs)
    o_ref = refs.pop()
    h = refs.pop(0)[...]
    if with_attn:
        a_ref, wo_ref = refs.pop(0), refs.pop(0)
        h = h + _dot(a_ref[...], wo_ref[...])
    g_ref, w1_ref, w3_ref, w2_ref = refs[:4]
    out = _ffn_body(h, g_ref[...], w1_ref, w3_ref, w2_ref)
    if with_final:
        out = _rms(out, refs[4][...])
    o_ref[...] = out


def _ffn(h, g, w1, w3, w2, attn=None, w_o=None, final_gain=None):
    t, d = h.shape
    row = lambda width: pl.BlockSpec((FFN_ROWS, width), lambda i: (i, 0))
    in_specs, args = [row(d)], [h]
    if attn is not None:
        in_specs += [row(attn.shape[1]), _resident(w_o.shape)]
        args += [attn, w_o]
    in_specs += [_resident((1, d)), _resident(w1.shape), _resident(w3.shape), _resident(w2.shape)]
    args += [g.reshape(1, d), w1, w3, w2]
    if final_gain is not None:
        in_specs.append(_resident((1, d)))
        args.append(final_gain.reshape(1, d))
    return pl.pallas_call(
        functools.partial(_ffn_kernel, with_attn=attn is not None,
                          with_final=final_gain is not None),
        grid=(t // FFN_ROWS,),
        in_specs=in_specs,
        out_specs=row(d),
        out_shape=jax.ShapeDtypeStruct((t, d), F32),
        compiler_params=_params(1),
        name="ffn" + ("_attn" if attn is not None else "") + ("_final" if final_gain is not None else ""),
    )(*args)


def _conv_kernel(h_ref, g_ref, wpw1_ref, wdw_ref, gn_ref, wpw2_ref, o_ref, u_ref, y_ref):
    rows, d = y_ref.shape
    n_lane_tiles = d // LANES

    @pl.when(pl.program_id(1) == 0)
    def _():
        u_ref[:, 0:CONV_TAIL, :] = jnp.zeros((n_lane_tiles, CONV_TAIL, LANES), F32)

    first = CONV_TAIL - (CONV_WIDTH - 1)
    for stage in range(rows // CONV_STAGE_ROWS):
        r_lo = stage * CONV_STAGE_ROWS
        stage_rows = slice(r_lo, r_lo + CONV_STAGE_ROWS)
        h = h_ref[0, stage_rows, :]
        m = _rms(h, g_ref[...]).astype(BF16)
        u = _dot(m, wpw1_ref[...])
        glu = u[:, :d] * jax.nn.sigmoid(u[:, d:])
        for c in range(n_lane_tiles):
            u_ref[c, CONV_TAIL + r_lo:CONV_TAIL + r_lo + CONV_STAGE_ROWS, :] = (
                glu[:, c * LANES:(c + 1) * LANES])

        for c in range(n_lane_tiles):
            lanes = slice(c * LANES, (c + 1) * LANES)
            for r0 in range(r_lo, r_lo + CONV_STAGE_ROWS, CONV_ROW_BLOCK):
                acc = wdw_ref[0:1, lanes] * u_ref[c, r0 + first:r0 + first + CONV_ROW_BLOCK, :]
                for k in range(1, CONV_WIDTH):
                    lo = r0 + first + k
                    acc = acc + wdw_ref[k:k + 1, lanes] * u_ref[c, lo:lo + CONV_ROW_BLOCK, :]
                y_ref[r0:r0 + CONV_ROW_BLOCK, lanes] = acc

        z = _rms(y_ref[stage_rows, :], gn_ref[...])
        z = (z * jax.nn.sigmoid(z)).astype(BF16)
        o_ref[0, stage_rows, :] = h + _dot(z, wpw2_ref[...])

    u_ref[:, 0:CONV_TAIL, :] = u_ref[:, rows:rows + CONV_TAIL, :]


def _conv_module(h, g, w_pw1, w_dw, g_norm, w_pw2):
    b, s, d = h.shape
    row = pl.BlockSpec((1, CONV_ROWS, d), lambda bi, si: (bi, si, 0))
    return pl.pallas_call(
        _conv_kernel,
        grid=(b, s // CONV_ROWS),
        in_specs=[row, _resident((1, d)), _resident((d, 2 * d)), _resident((CONV_WIDTH, d)),
                  _resident((1, d)), _resident((d, d))],
        out_specs=row,
        out_shape=jax.ShapeDtypeStruct((b, s, d), F32),
        scratch_shapes=[pltpu.VMEM((d // LANES, CONV_TAIL + CONV_ROWS, LANES), F32),
                        pltpu.VMEM((CONV_ROWS, d), F32)],
        compiler_params=_params(2),
        name="conv_module",
    )(h, g.reshape(1, d), w_pw1, w_dw, g_norm.reshape(1, d), w_pw2)


def _dot_nt(a, b):
    return lax.dot_general(a, b, (((1,), (1,)), ((), ())), preferred_element_type=F32)


def _mla_proj_kernel(h_ref, pos_ref, freq_ref, g_ref, wa_ref,
                     gq_ref, gkv_ref, wqn_ref, wqr_ref, wqs_ref, wk_ref, wv_ref,
                     qt_ref, k_ref, vt_ref):
    m = _rms(h_ref[0], g_ref[...]).astype(BF16)
    a = _dot(m, wa_ref[...])
    c_q = _rms(a[:, :Q_LORA], gq_ref[...]).astype(BF16)
    c_kv = _rms(a[:, Q_LORA:Q_LORA + KV_LORA], gkv_ref[...]).astype(BF16)
    kr = a[:, Q_LORA + KV_LORA:Q_LORA + KV_LORA + LANES]
    kr_swapped = a[:, Q_LORA + KV_LORA + LANES:]

    ang = freq_ref[...] * pos_ref[0]
    cos, sin = jnp.cos(ang), jnp.sin(ang)
    pad = jnp.zeros((LANES - QK_ROPE, ang.shape[1]), F32)
    cos_q = jnp.concatenate([cos, cos], axis=0)
    sin_q = jnp.concatenate([-sin, sin], axis=0)
    cos_k = jnp.concatenate([cos_q, pad], axis=0).T
    sin_k = jnp.concatenate([sin_q, pad], axis=0).T

    scale = (QK_NOPE + QK_ROPE) ** -0.5 * LOG2_E
    qt_nope = _dot_nt(wqn_ref[...], c_q)
    qt_rope = _dot_nt(wqr_ref[...], c_q)
    qt_rope_swapped = _dot_nt(wqs_ref[...], c_q)
    k_nope = _dot(c_kv, wk_ref[...])
    vt = _dot_nt(wv_ref[...], c_kv)
    k_rot = (kr * cos_k + kr_swapped * sin_k).astype(BF16)
    aug_row = lax.broadcasted_iota(jnp.int32, (V_AUG - V_HEAD, vt.shape[1]), 0)
    sum_rows = jnp.where(aug_row == 0, 1.0, 0.0).astype(BF16)
    q_pad = jnp.zeros((HEAD_PAD - QK_NOPE - QK_ROPE, vt.shape[1]), BF16)
    for hd in range(N_HEADS):
        sl = slice(hd * LANES, (hd + 1) * LANES)
        rope_rows = slice(hd * QK_ROPE, (hd + 1) * QK_ROPE)
        qt_rot = qt_rope[rope_rows, :] * cos_q + qt_rope_swapped[rope_rows, :] * sin_q
        qt_ref[0, hd, 0:QK_NOPE, :] = (qt_nope[sl, :] * scale).astype(BF16)
        qt_ref[0, hd, QK_NOPE:QK_NOPE + QK_ROPE, :] = (qt_rot * scale).astype(BF16)
        qt_ref[0, hd, QK_NOPE + QK_ROPE:HEAD_PAD, :] = q_pad
        k_ref[0, hd, :, 0:LANES] = k_nope[:, sl].astype(BF16)
        k_ref[0, hd, :, LANES:HEAD_PAD] = k_rot
        vt_ref[0, hd, 0, 0:V_HEAD, :] = vt[sl, :].astype(BF16)
        vt_ref[0, hd, 0, V_HEAD:V_AUG, :] = sum_rows


def _rope_tile_weights(w_rope):
    c, g, _ = w_rope.shape
    half = QK_ROPE // 2
    x1, x2 = w_rope[..., :half], w_rope[..., half:]
    pad = jnp.zeros((c, g, LANES - QK_ROPE), w_rope.dtype)
    plain = jnp.concatenate([x1, x2, pad], axis=-1).reshape(c, g * LANES)
    swapped = jnp.concatenate([x2, x1, pad], axis=-1).reshape(c, g * LANES)
    return plain, swapped


def _mla_proj(h, pos, g, w_a, g_q, g_kv, w_uq, w_ukv):
    b, s, d = h.shape
    kr_plain, kr_swapped = _rope_tile_weights(w_a[:, None, Q_LORA + KV_LORA:])
    wa = jnp.concatenate([w_a[:, :Q_LORA + KV_LORA], kr_plain, kr_swapped], axis=1).astype(BF16)
    wqn_t = w_uq[:, :, :QK_NOPE].reshape(Q_LORA, N_HEADS * QK_NOPE).T.astype(BF16)
    half = QK_ROPE // 2
    w_rope = w_uq[:, :, QK_NOPE:]
    w_rope_swapped = jnp.concatenate([w_rope[..., half:], w_rope[..., :half]], axis=-1)
    wqr_t = w_rope.reshape(Q_LORA, N_HEADS * QK_ROPE).T.astype(BF16)
    wqs_t = w_rope_swapped.reshape(Q_LORA, N_HEADS * QK_ROPE).T.astype(BF16)
    wk = w_ukv[:, :, :QK_NOPE].reshape(KV_LORA, N_HEADS * QK_NOPE).astype(BF16)
    wv_t = w_ukv[:, :, QK_NOPE:].reshape(KV_LORA, N_HEADS * V_HEAD).T.astype(BF16)
    inv_freq = ROPE_THETA ** (-2.0 * jnp.arange(QK_ROPE // 2, dtype=F32) / QK_ROPE)
    pos_f = pos.astype(F32)

    assert PROJ_ROWS == ATTN_BLOCK
    row = pl.BlockSpec((1, PROJ_ROWS, d), lambda bi, si: (bi, si, 0))
    return pl.pallas_call(
        _mla_proj_kernel,
        grid=(b, s // PROJ_ROWS),
        in_specs=[row,
                  pl.BlockSpec((1, 1, PROJ_ROWS), lambda bi, si: (bi, 0, si)),
                  _resident((QK_ROPE // 2, 1)), _resident((1, d)),
                  _resident(wa.shape), _resident((1, Q_LORA)), _resident((1, KV_LORA)),
                  _resident(wqn_t.shape), _resident(wqr_t.shape), _resident(wqs_t.shape),
                  _resident(wk.shape), _resident(wv_t.shape)],
        out_specs=[pl.BlockSpec((1, N_HEADS, HEAD_PAD, PROJ_ROWS), lambda bi, si: (bi, 0, 0, si)),
                   pl.BlockSpec((1, N_HEADS, PROJ_ROWS, HEAD_PAD), lambda bi, si: (bi, 0, si, 0)),
                   pl.BlockSpec((1, N_HEADS, 1, V_AUG, PROJ_ROWS),
                                lambda bi, si: (bi, 0, si, 0, 0))],
        out_shape=[jax.ShapeDtypeStruct((b, N_HEADS, HEAD_PAD, s), BF16),
                   jax.ShapeDtypeStruct((b, N_HEADS, s, HEAD_PAD), BF16),
                   jax.ShapeDtypeStruct((b, N_HEADS, s // PROJ_ROWS, V_AUG, PROJ_ROWS), BF16)],
        compiler_params=_params(2),
        name="mla_proj",
    )(h, pos_f[:, None, :], inv_freq.reshape(QK_ROPE // 2, 1), g.reshape(1, d), wa,
      g_q.reshape(1, Q_LORA),
      g_kv.reshape(1, KV_LORA), wqn_t, wqr_t, wqs_t, wk, wv_t)


def _sublane_all(x, op):
    for shift in (4, 2, 1):
        x = op(x, pltpu.roll(x, shift, axis=0))
    return x


def _attn_kernel(qt_ref, k_ref, vt_ref, bias_ref, o_ref, s0_ref, s1_ref, mx0_ref, mx1_ref,
                 m_ref, acc_ref):
    qb, kb = ATTN_Q_BLOCK, ATTN_BLOCK
    n_sub = qb // kb
    qi = pl.program_id(2)
    bufs = ((s0_ref, mx0_ref), (s1_ref, mx1_ref))

    def keys(j):
        return k_ref[0, 0, pl.ds(pl.multiple_of(j * kb, kb), kb), :]

    def slabs(x):
        return x.reshape(x.shape[0] // SUBLANES, SUBLANES, x.shape[1])

    def raw_scores(j, buf, lo=0, hi=qb, masked=False):
        s_ref, mx_ref = buf
        scores = _dot(keys(j), qt_ref[0, 0, :, lo:hi])
        if masked:
            scores = scores + bias_ref[:, lo % kb:lo % kb + hi - lo]
        s_ref[:, lo:hi] = scores
        mx_ref[:, lo:hi] = jnp.max(slabs(s_ref[:, lo:hi]), axis=0)

    def update(j, buf, lo=0, hi=qb):
        s_ref, mx_ref = buf
        m_old = m_ref[:, lo:hi]
        m_new = jnp.maximum(m_old, _sublane_all(mx_ref[:, lo:hi], jnp.maximum))
        alpha = jnp.exp2(m_old - m_new)
        p = jnp.exp2(slabs(s_ref[:, lo:hi]) - m_new[None])
        m_ref[:, lo:hi] = m_new
        pv = _dot(vt_ref[0, 0, j], p.reshape(kb, hi - lo).astype(BF16))
        acc_ref[:, lo:hi] = (slabs(acc_ref[:, lo:hi]) * alpha[None] + slabs(pv)).reshape(pv.shape)

    def trip(i, carry):
        j = n_sub * i
        for t in range(n_sub):
            for lo in range(0, qb, ATTN_COL_CHUNK):
                raw_scores(j + t + 1, bufs[(t + 1) % 2], lo, lo + ATTN_COL_CHUNK)
                update(j + t, bufs[t % 2], lo, lo + ATTN_COL_CHUNK)
        return carry

    acc_ref[...] = jnp.zeros(acc_ref.shape, F32)
    m_ref[...] = jnp.full(m_ref.shape, -jnp.inf, F32)
    raw_scores(0, bufs[0])
    lax.fori_loop(0, qi, trip, 0)

    j = n_sub * qi
    s0_ref[:, 0:kb] = s0_ref[:, 0:kb] + bias_ref[...]
    mx0_ref[:, 0:kb] = jnp.max(slabs(s0_ref[:, 0:kb]), axis=0)
    chunks_per_block = kb // ATTN_COL_CHUNK
    for t in range(n_sub):
        for c in range(t * chunks_per_block, n_sub * chunks_per_block):
            lo = c * ATTN_COL_CHUNK
            if c // chunks_per_block >= t + 1:
                raw_scores(j + t + 1, bufs[(t + 1) % 2], lo, lo + ATTN_COL_CHUNK,
                           masked=c // chunks_per_block == t + 1)
            update(j + t, bufs[t % 2], lo, lo + ATTN_COL_CHUNK)
    o_ref[0] = (acc_ref[0:V_HEAD, :] / acc_ref[V_HEAD:V_HEAD + 1, :]).T.astype(BF16)


def _attention(qt, k, vt):
    b, n_heads, s, _ = k.shape
    assert ATTN_Q_BLOCK % (2 * ATTN_BLOCK) == 0 and ATTN_BLOCK % CHUNK == 0
    chunk_of = jnp.arange(ATTN_BLOCK, dtype=jnp.int32) // CHUNK
    bias = jnp.where(chunk_of[:, None] <= chunk_of[None, :], 0.0, -jnp.inf).astype(F32)
    return pl.pallas_call(
        _attn_kernel,
        grid=(b, n_heads, s // ATTN_Q_BLOCK),
        in_specs=[pl.BlockSpec((1, 1, HEAD_PAD, ATTN_Q_BLOCK), lambda bi, hi, qi: (bi, hi, 0, qi)),
                  pl.BlockSpec((1, 1, s, HEAD_PAD), lambda bi, hi, qi: (bi, hi, 0, 0)),
                  pl.BlockSpec((1, 1, s // ATTN_BLOCK, V_AUG, ATTN_BLOCK),
                               lambda bi, hi, qi: (bi, hi, 0, 0, 0)),
                  _resident((ATTN_BLOCK, ATTN_BLOCK))],
        out_specs=pl.BlockSpec((1, ATTN_Q_BLOCK, V_HEAD), lambda bi, hi, qi: (bi, qi, hi)),
        out_shape=jax.ShapeDtypeStruct((b, s, n_heads * V_HEAD), BF16),
        scratch_shapes=[pltpu.VMEM((ATTN_BLOCK, ATTN_Q_BLOCK), F32),
                        pltpu.VMEM((ATTN_BLOCK, ATTN_Q_BLOCK), F32),
                        pltpu.VMEM((SUBLANES, ATTN_Q_BLOCK), F32),
                        pltpu.VMEM((SUBLANES, ATTN_Q_BLOCK), F32),
                        pltpu.VMEM((SUBLANES, ATTN_Q_BLOCK), F32),
                        pltpu.VMEM((V_AUG, ATTN_Q_BLOCK), F32)],
        compiler_params=_params(3),
        name="mla_attention",
    )(qt, k, vt, bias)


def kernel(x, positions, ffn_norm1, ffn1_w1, ffn1_w3, ffn1_w2, mix_norm, ffn_norm2, ffn2_w1, ffn2_w3, ffn2_w2, conv_w_pw1, conv_w_dw, conv_norm, conv_w_pw2, mla_w_a, mla_q_norm, mla_kv_norm, mla_w_uq, mla_w_ukv, mla_w_o, final_norm):
    b, s, d = x.shape
    depth = ffn_norm1.shape[0]
    n_mixers = 2
    bf = lambda w: w.astype(BF16)
    ffn1_w1, ffn1_w3, ffn1_w2 = bf(ffn1_w1), bf(ffn1_w3), bf(ffn1_w2)
    ffn2_w1, ffn2_w3, ffn2_w2 = bf(ffn2_w1), bf(ffn2_w3), bf(ffn2_w2)
    h = x
    for i in range(depth):
        h = _ffn(h.reshape(b * s, d), ffn_norm1[i], ffn1_w1[i], ffn1_w3[i],
                 ffn1_w2[i]).reshape(b, s, d)
        j = i // n_mixers
        attn, w_o = None, None
        if i % n_mixers == 0:
            h = _conv_module(h, mix_norm[i], bf(conv_w_pw1[j]), conv_w_dw[j], conv_norm[j],
                             bf(conv_w_pw2[j]))
        else:
            qt, k, vt = _mla_proj(h, positions, mix_norm[i], mla_w_a[j], mla_q_norm[j],
                                  mla_kv_norm[j], mla_w_uq[j], mla_w_ukv[j])
            attn, w_o = _attention(qt, k, vt).reshape(b * s, -1), bf(mla_w_o[j])
        last = i == depth - 1
        h = _ffn(h.reshape(b * s, d), ffn_norm2[i], ffn2_w1[i], ffn2_w3[i],
                 ffn2_w2[i], attn=attn, w_o=w_o,
                 final_gain=final_norm if last else None).reshape(b, s, d)
    return h
```

```python
import functools

import jax
import jax.numpy as jnp
from jax import lax
from jax.experimental import pallas as pl
from jax.experimental.pallas import tpu as pltpu

CHUNK = 64
CONV_WIDTH = 31
N_HEADS = 8
QK_NOPE = 128
QK_ROPE = 64
V_HEAD = 128
V_AUG = V_HEAD + 16
Q_LORA = 512
KV_LORA = 256
ROPE_THETA = 10000.0
RMS_EPS = 1e-6
FFN_RES_WEIGHT = 0.5
LOG2_E = 1.4426950408889634

LANES = 128
SUBLANES = 8
VMEM_LIMIT_BYTES = 56 * 1024 * 1024

FFN_ROWS = 512
CONV_ROWS = 512
CONV_TAIL = 32
CONV_ROW_BLOCK = 128
CONV_STAGE_ROWS = 256
PROJ_ROWS = 512
ATTN_BLOCK = 512
ATTN_Q_BLOCK = 4096
ATTN_COL_CHUNK = 256
HEAD_PAD = 2 * LANES

BF16 = jnp.bfloat16
F32 = jnp.float32


def _rms(x, g):
    return x * lax.rsqrt(jnp.mean(x * x, axis=-1, keepdims=True) + RMS_EPS) * g


def _dot(a, b):
    return jnp.dot(a, b, preferred_element_type=F32)


def _resident(shape):
    zeros = (0,) * len(shape)
    return pl.BlockSpec(shape, lambda *_: zeros, pipeline_mode=pl.Buffered(1))


def _params(n_axes):
    return pltpu.CompilerParams(
        dimension_semantics=("arbitrary",) * n_axes,
        vmem_limit_bytes=VMEM_LIMIT_BYTES)


def _ffn_body(h, g, w1_ref, w3_ref, w2_ref):
    x = _rms(h, g).astype(BF16)
    gate = _dot(x, w1_ref[...])
    up = _dot(x, w3_ref[...])
    a = (gate * jax.nn.sigmoid(gate) * up).astype(BF16)
    return h + FFN_RES_WEIGHT * _dot(a, w2_ref[...])


def _ffn_kernel(*refs, with_attn, with_final):
    refs = list(refs)
    o_ref = refs.pop()
    h = refs.pop(0)[...]
    if with_attn:
        a_ref, wo_ref = refs.pop(0), refs.pop(0)
        h = h + _dot(a_ref[...], wo_ref[...])
    g_ref, w1_ref, w3_ref, w2_ref = refs[:4]
    out = _ffn_body(h, g_ref[...], w1_ref, w3_ref, w2_ref)
    if with_final:
        out = _rms(out, refs[4][...])
    o_ref[...] = out


def _ffn(h, g, w1, w3, w2, attn=None, w_o=None, final_gain=None):
    t, d = h.shape
    row = lambda width: pl.BlockSpec((FFN_ROWS, width), lambda i: (i, 0))
    in_specs, args = [row(d)], [h]
    if attn is not None:
        in_specs += [row(attn.shape[1]), _resident(w_o.shape)]
        args += [attn, w_o]
    in_specs += [_resident((1, d)), _resident(w1.shape), _resident(w3.shape), _resident(w2.shape)]
    args += [g.reshape(1, d), w1, w3, w2]
    if final_gain is not None:
        in_specs.append(_resident((1, d)))
        args.append(final_gain.reshape(1, d))
    return pl.pallas_call(
        functools.partial(_ffn_kernel, with_attn=attn is not None,
                          with_final=final_gain is not None),
        grid=(t // FFN_ROWS,),
        in_specs=in_specs,
        out_specs=row(d),
        out_shape=jax.ShapeDtypeStruct((t, d), F32),
        compiler_params=_params(1),
        name="ffn" + ("_attn" if attn is not None else "") + ("_final" if final_gain is not None else ""),
    )(*args)


def _conv_kernel(h_ref, g_ref, wpw1_ref, wdw_ref, gn_ref, wpw2_ref, o_ref, u_ref, y_ref):
    rows, d = y_ref.shape
    n_lane_tiles = d // LANES

    @pl.when(pl.program_id(1) == 0)
    def _():
        u_ref[:, 0:CONV_TAIL, :] = jnp.zeros((n_lane_tiles, CONV_TAIL, LANES), F32)

    first = CONV_TAIL - (CONV_WIDTH - 1)
    for stage in range(rows // CONV_STAGE_ROWS):
        r_lo = stage * CONV_STAGE_ROWS
        stage_rows = slice(r_lo, r_lo + CONV_STAGE_ROWS)
        h = h_ref[0, stage_rows, :]
        m = _rms(h, g_ref[...]).astype(BF16)
        u = _dot(m, wpw1_ref[...])
        glu = u[:, :d] * jax.nn.sigmoid(u[:, d:])
        for c in range(n_lane_tiles):
            u_ref[c, CONV_TAIL + r_lo:CONV_TAIL + r_lo + CONV_STAGE_ROWS, :] = (
                glu[:, c * LANES:(c + 1) * LANES])

        for c in range(n_lane_tiles):
            lanes = slice(c * LANES, (c + 1) * LANES)
            for r0 in range(r_lo, r_lo + CONV_STAGE_ROWS, CONV_ROW_BLOCK):
                acc = wdw_ref[0:1, lanes] * u_ref[c, r0 + first:r0 + first + CONV_ROW_BLOCK, :]
                for k in range(1, CONV_WIDTH):
                    lo = r0 + first + k
                    acc = acc + wdw_ref[k:k + 1, lanes] * u_ref[c, lo:lo + CONV_ROW_BLOCK, :]
                y_ref[r0:r0 + CONV_ROW_BLOCK, lanes] = acc

        z = _rms(y_ref[stage_rows, :], gn_ref[...])
        z = (z * jax.nn.sigmoid(z)).astype(BF16)
        o_ref[0, stage_rows, :] = h + _dot(z, wpw2_ref[...])

    u_ref[:, 0:CONV_TAIL, :] = u_ref[:, rows:rows + CONV_TAIL, :]


def _conv_module(h, g, w_pw1, w_dw, g_norm, w_pw2):
    b, s, d = h.shape
    row = pl.BlockSpec((1, CONV_ROWS, d), lambda bi, si: (bi, si, 0))
    return pl.pallas_call(
        _conv_kernel,
        grid=(b, s // CONV_ROWS),
        in_specs=[row, _resident((1, d)), _resident((d, 2 * d)), _resident((CONV_WIDTH, d)),
                  _resident((1, d)), _resident((d, d))],
        out_specs=row,
        out_shape=jax.ShapeDtypeStruct((b, s, d), F32),
        scratch_shapes=[pltpu.VMEM((d // LANES, CONV_TAIL + CONV_ROWS, LANES), F32),
                        pltpu.VMEM((CONV_ROWS, d), F32)],
        compiler_params=_params(2),
        name="conv_module",
    )(h, g.reshape(1, d), w_pw1, w_dw, g_norm.reshape(1, d), w_pw2)


def _dot_nt(a, b):
    return lax.dot_general(a, b, (((1,), (1,)), ((), ())), preferred_element_type=F32)


def _mla_proj_kernel(h_ref, pos_ref, freq_ref, g_ref, wa_ref,
                     gq_ref, gkv_ref, wqn_ref, wqr_ref, wqs_ref, wk_ref, wv_ref,
                     qt_ref, k_ref, vt_ref):
    m = _rms(h_ref[0], g_ref[...]).astype(BF16)
    a = _dot(m, wa_ref[...])
    c_q = _rms(a[:, :Q_LORA], gq_ref[...]).astype(BF16)
    c_kv = _rms(a[:, Q_LORA:Q_LORA + KV_LORA], gkv_ref[...]).astype(BF16)
    kr = a[:, Q_LORA + KV_LORA:Q_LORA + KV_LORA + LANES]
    kr_swapped = a[:, Q_LORA + KV_LORA + LANES:]

    ang = freq_ref[...] * pos_ref[0]
    cos, sin = jnp.cos(ang), jnp.sin(ang)
    pad = jnp.zeros((LANES - QK_ROPE, ang.shape[1]), F32)
    cos_q = jnp.concatenate([cos, cos], axis=0)
    sin_q = jnp.concatenate([-sin, sin], axis=0)
    cos_k = jnp.concatenate([cos_q, pad], axis=0).T
    sin_k = jnp.concatenate([sin_q, pad], axis=0).T

    scale = (QK_NOPE + QK_ROPE) ** -0.5 * LOG2_E
    qt_nope = _dot_nt(wqn_ref[...], c_q)
    qt_rope = _dot_nt(wqr_ref[...], c_q)
    qt_rope_swapped = _dot_nt(wqs_ref[...], c_q)
    k_nope = _dot(c_kv, wk_ref[...])
    vt = _dot_nt(wv_ref[...], c_kv)
    k_rot = (kr * cos_k + kr_swapped * sin_k).astype(BF16)
    aug_row = lax.broadcasted_iota(jnp.int32, (V_AUG - V_HEAD, vt.shape[1]), 0)
    sum_rows = jnp.where(aug_row == 0, 1.0, 0.0).astype(BF16)
    q_pad = jnp.zeros((HEAD_PAD - QK_NOPE - QK_ROPE, vt.shape[1]), BF16)
    for hd in range(N_HEADS):
        sl = slice(hd * LANES, (hd + 1) * LANES)
        rope_rows = slice(hd * QK_ROPE, (hd + 1) * QK_ROPE)
        qt_rot = qt_rope[rope_rows, :] * cos_q + qt_rope_swapped[rope_rows, :] * sin_q
        qt_ref[0, hd, 0:QK_NOPE, :] = (qt_nope[sl, :] * scale).astype(BF16)
        qt_ref[0, hd, QK_NOPE:QK_NOPE + QK_ROPE, :] = (qt_rot * scale).astype(BF16)
        qt_ref[0, hd, QK_NOPE + QK_ROPE:HEAD_PAD, :] = q_pad
        k_ref[0, hd, :, 0:LANES] = k_nope[:, sl].astype(BF16)
        k_ref[0, hd, :, LANES:HEAD_PAD] = k_rot
        vt_ref[0, hd, 0, 0:V_HEAD, :] = vt[sl, :].astype(BF16)
        vt_ref[0, hd, 0, V_HEAD:V_AUG, :] = sum_rows


def _rope_tile_weights(w_rope):
    c, g, _ = w_rope.shape
    half = QK_ROPE // 2
    x1, x2 = w_rope[..., :half], w_rope[..., half:]
    pad = jnp.zeros((c, g, LANES - QK_ROPE), w_rope.dtype)
    plain = jnp.concatenate([x1, x2, pad], axis=-1).reshape(c, g * LANES)
    swapped = jnp.concatenate([x2, x1, pad], axis=-1).reshape(c, g * LANES)
    return plain, swapped


def _mla_proj(h, pos, g, w_a, g_q, g_kv, w_uq, w_ukv):
    b, s, d = h.shape
    kr_plain, kr_swapped = _rope_tile_weights(w_a[:, None, Q_LORA + KV_LORA:])
    wa = jnp.concatenate([w_a[:, :Q_LORA + KV_LORA], kr_plain, kr_swapped], axis=1).astype(BF16)
    wqn_t = w_uq[:, :, :QK_NOPE].reshape(Q_LORA, N_HEADS * QK_NOPE).T.astype(BF16)
    half = QK_ROPE // 2
    w_rope = w_uq[:, :, QK_NOPE:]
    w_rope_swapped = jnp.concatenate([w_rope[..., half:], w_rope[..., :half]], axis=-1)
    wqr_t = w_rope.reshape(Q_LORA, N_HEADS * QK_ROPE).T.astype(BF16)
    wqs_t = w_rope_swapped.reshape(Q_LORA, N_HEADS * QK_ROPE).T.astype(BF16)
    wk = w_ukv[:, :, :QK_NOPE].reshape(KV_LORA, N_HEADS * QK_NOPE).astype(BF16)
    wv_t = w_ukv[:, :, QK_NOPE:].reshape(KV_LORA, N_HEADS * V_HEAD).T.astype(BF16)
    inv_freq = ROPE_THETA ** (-2.0 * jnp.arange(QK_ROPE // 2, dtype=F32) / QK_ROPE)
    pos_f = pos.astype(F32)

    assert PROJ_ROWS == ATTN_BLOCK
    row = pl.BlockSpec((1, PROJ_ROWS, d), lambda bi, si: (bi, si, 0))
    return pl.pallas_call(
        _mla_proj_kernel,
        grid=(b, s // PROJ_ROWS),
        in_specs=[row,
                  pl.BlockSpec((1, 1, PROJ_ROWS), lambda bi, si: (bi, 0, si)),
                  _resident((QK_ROPE // 2, 1)), _resident((1, d)),
                  _resident(wa.shape), _resident((1, Q_LORA)), _resident((1, KV_LORA)),
                  _resident(wqn_t.shape), _resident(wqr_t.shape), _resident(wqs_t.shape),
                  _resident(wk.shape), _resident(wv_t.shape)],
        out_specs=[pl.BlockSpec((1, N_HEADS, HEAD_PAD, PROJ_ROWS), lambda bi, si: (bi, 0, 0, si)),
                   pl.BlockSpec((1, N_HEADS, PROJ_ROWS, HEAD_PAD), lambda bi, si: (bi, 0, si, 0)),
                   pl.BlockSpec((1, N_HEADS, 1, V_AUG, PROJ_ROWS),
                                lambda bi, si: (bi, 0, si, 0, 0))],
        out_shape=[jax.ShapeDtypeStruct((b, N_HEADS, HEAD_PAD, s), BF16),
                   jax.ShapeDtypeStruct((b, N_HEADS, s, HEAD_PAD), BF16),
                   jax.ShapeDtypeStruct((b, N_HEADS, s // PROJ_ROWS, V_AUG, PROJ_ROWS), BF16)],
        compiler_params=_params(2),
        name="mla_proj",
    )(h, pos_f[:, None, :], inv_freq.reshape(QK_ROPE // 2, 1), g.reshape(1, d), wa,
      g_q.reshape(1, Q_LORA),
      g_kv.reshape(1, KV_LORA), wqn_t, wqr_t, wqs_t, wk, wv_t)


def _sublane_all(x, op):
    for shift in (4, 2, 1):
        x = op(x, pltpu.roll(x, shift, axis=0))
    return x


def _attn_kernel(qt_ref, k_ref, vt_ref, bias_ref, o_ref, s0_ref, s1_ref, mx0_ref, mx1_ref,
                 m_ref, acc_ref):
    qb, kb = ATTN_Q_BLOCK, ATTN_BLOCK
    n_sub = qb // kb
    qi = pl.program_id(2)
    bufs = ((s0_ref, mx0_ref), (s1_ref, mx1_ref))

    def keys(j):
        return k_ref[0, 0, pl.ds(pl.multiple_of(j * kb, kb), kb), :]

    def slabs(x):
        return x.reshape(x.shape[0] // SUBLANES, SUBLANES, x.shape[1])

    def raw_scores(j, buf, lo=0, hi=qb, masked=False):
        s_ref, mx_ref = buf
        scores = _dot(keys(j), qt_ref[0, 0, :, lo:hi])
        if masked:
            scores = scores + bias_ref[:, lo % kb:lo % kb + hi - lo]
        s_ref[:, lo:hi] = scores
        mx_ref[:, lo:hi] = jnp.max(slabs(s_ref[:, lo:hi]), axis=0)

    def update(j, buf, lo=0, hi=qb):
        s_ref, mx_ref = buf
        m_old = m_ref[:, lo:hi]
        m_new = jnp.maximum(m_old, _sublane_all(mx_ref[:, lo:hi], jnp.maximum))
        alpha = jnp.exp2(m_old - m_new)
        p = jnp.exp2(slabs(s_ref[:, lo:hi]) - m_new[None])
        m_ref[:, lo:hi] = m_new
        pv = _dot(vt_ref[0, 0, j], p.reshape(kb, hi - lo).astype(BF16))
        acc_ref[:, lo:hi] = (slabs(acc_ref[:, lo:hi]) * alpha[None] + slabs(pv)).reshape(pv.shape)

    def trip(i, carry):
        j = n_sub * i
        for t in range(n_sub):
            for lo in range(0, qb, ATTN_COL_CHUNK):
                raw_scores(j + t + 1, bufs[(t + 1) % 2], lo, lo + ATTN_COL_CHUNK)
                update(j + t, bufs[t % 2], lo, lo + ATTN_COL_CHUNK)
        return carry

    acc_ref[...] = jnp.zeros(acc_ref.shape, F32)
    m_ref[...] = jnp.full(m_ref.shape, -jnp.inf, F32)
    raw_scores(0, bufs[0])
    lax.fori_loop(0, qi, trip, 0)

    j = n_sub * qi
    s0_ref[:, 0:kb] = s0_ref[:, 0:kb] + bias_ref[...]
    mx0_ref[:, 0:kb] = jnp.max(slabs(s0_ref[:, 0:kb]), axis=0)
    chunks_per_block = kb // ATTN_COL_CHUNK
    for t in range(n_sub):
        for c in range(t * chunks_per_block, n_sub * chunks_per_block):
            lo = c * ATTN_COL_CHUNK
            if c // chunks_per_block >= t + 1:
                raw_scores(j + t + 1, bufs[(t + 1) % 2], lo, lo + ATTN_COL_CHUNK,
                           masked=c // chunks_per_block == t + 1)
            update(j + t, bufs[t % 2], lo, lo + ATTN_COL_CHUNK)
    o_ref[0] = (acc_ref[0:V_HEAD, :] / acc_ref[V_HEAD:V_HEAD + 1, :]).T.astype(BF16)


def _attention(qt, k, vt):
    b, n_heads, s, _ = k.shape
    assert ATTN_Q_BLOCK % (2 * ATTN_BLOCK) == 0 and ATTN_BLOCK % CHUNK == 0
    chunk_of = jnp.arange(ATTN_BLOCK, dtype=jnp.int32) // CHUNK
    bias = jnp.where(chunk_of[:, None] <= chunk_of[None, :], 0.0, -jnp.inf).astype(F32)
    return pl.pallas_call(
        _attn_kernel,
        grid=(b, n_heads, s // ATTN_Q_BLOCK),
        in_specs=[pl.BlockSpec((1, 1, HEAD_PAD, ATTN_Q_BLOCK), lambda bi, hi, qi: (bi, hi, 0, qi)),
                  pl.BlockSpec((1, 1, s, HEAD_PAD), lambda bi, hi, qi: (bi, hi, 0, 0)),
                  pl.BlockSpec((1, 1, s // ATTN_BLOCK, V_AUG, ATTN_BLOCK),
                               lambda bi, hi, qi: (bi, hi, 0, 0, 0)),
                  _resident((ATTN_BLOCK, ATTN_BLOCK))],
        out_specs=pl.BlockSpec((1, ATTN_Q_BLOCK, V_HEAD), lambda bi, hi, qi: (bi, qi, hi)),
        out_shape=jax.ShapeDtypeStruct((b, s, n_heads * V_HEAD), BF16),
        scratch_shapes=[pltpu.VMEM((ATTN_BLOCK, ATTN_Q_BLOCK), F32),
                        pltpu.VMEM((ATTN_BLOCK, ATTN_Q_BLOCK), F32),
                        pltpu.VMEM((SUBLANES, ATTN_Q_BLOCK), F32),
                        pltpu.VMEM((SUBLANES, ATTN_Q_BLOCK), F32),
                        pltpu.VMEM((SUBLANES, ATTN_Q_BLOCK), F32),
                        pltpu.VMEM((V_AUG, ATTN_Q_BLOCK), F32)],
        compiler_params=_params(3),
        name="mla_attention",
    )(qt, k, vt, bias)


def kernel(x, positions, ffn_norm1, ffn1_w1, ffn1_w3, ffn1_w2, mix_norm, ffn_norm2, ffn2_w1, ffn2_w3, ffn2_w2, conv_w_pw1, conv_w_dw, conv_norm, conv_w_pw2, mla_w_a, mla_q_norm, mla_kv_norm, mla_w_uq, mla_w_ukv, mla_w_o, final_norm):
    b, s, d = x.shape
    depth = ffn_norm1.shape[0]
    n_mixers = 2
    bf = lambda w: w.astype(BF16)
    ffn1_w1, ffn1_w3, ffn1_w2 = bf(ffn1_w1), bf(ffn1_w3), bf(ffn1_w2)
    ffn2_w1, ffn2_w3, ffn2_w2 = bf(ffn2_w1), bf(ffn2_w3), bf(ffn2_w2)
    h = x
    for i in range(depth):
        h = _ffn(h.reshape(b * s, d), ffn_norm1[i], ffn1_w1[i], ffn1_w3[i],
                 ffn1_w2[i]).reshape(b, s, d)
        j = i // n_mixers
        attn, w_o = None, None
        if i % n_mixers == 0:
            h = _conv_module(h, mix_norm[i], bf(conv_w_pw1[j]), conv_w_dw[j], conv_norm[j],
                             bf(conv_w_pw2[j]))
        else:
            qt, k, vt = _mla_proj(h, positions, mix_norm[i], mla_w_a[j], mla_q_norm[j],
                                  mla_kv_norm[j], mla_w_uq[j], mla_w_ukv[j])
            attn, w_o = _attention(qt, k, vt).reshape(b * s, -1), bf(mla_w_o[j])
        last = i == depth - 1
        h = _ffn(h.reshape(b * s, d), ffn_norm2[i], ffn2_w1[i], ffn2_w3[i],
                 ffn2_w2[i], attn=attn, w_o=w_o,
                 final_gain=final_norm if last else None).reshape(b, s, d)
    return h
```

```python
import functools

import jax
import jax.numpy as jnp
from jax import lax
from jax.experimental import pallas as pl
from jax.experimental.pallas import tpu as pltpu

CHUNK = 64
CONV_WIDTH = 31
N_HEADS = 8
QK_NOPE = 128
QK_ROPE = 64
V_HEAD = 128
V_AUG = V_HEAD + 16
Q_LORA = 512
KV_LORA = 256
ROPE_THETA = 10000.0
RMS_EPS = 1e-6
FFN_RES_WEIGHT = 0.5
LOG2_E = 1.4426950408889634

LANES = 128
SUBLANES = 8
VMEM_LIMIT_BYTES = 56 * 1024 * 1024

FFN_ROWS = 512
FFN_SUBTILES = 4
CONV_ROWS = 512
CONV_TAIL = 32
CONV_ROW_BLOCK = 128
CONV_STAGE_ROWS = 512
PROJ_ROWS = 512
ATTN_BLOCK = 512
ATTN_Q_BLOCK = 4096
ATTN_COL_CHUNK = 256
HEAD_PAD = 2 * LANES

BF16 = jnp.bfloat16
F32 = jnp.float32


def _rms(x, g):
    return x * lax.rsqrt(jnp.mean(x * x, axis=-1, keepdims=True) + RMS_EPS) * g


def _dot(a, b):
    return jnp.dot(a, b, preferred_element_type=F32)


def _resident(shape):
    zeros = (0,) * len(shape)
    return pl.BlockSpec(shape, lambda *_: zeros, pipeline_mode=pl.Buffered(1))


def _params(n_axes):
    return pltpu.CompilerParams(
        dimension_semantics=("arbitrary",) * n_axes,
        vmem_limit_bytes=VMEM_LIMIT_BYTES)


def _ffn_body(h, g, w1_ref, w3_ref, w2_ref):
    q = h.shape[0] // FFN_SUBTILES
    hs = [h[i * q:(i + 1) * q] for i in range(FFN_SUBTILES)]
    xs = [_rms(hh, g).astype(BF16) for hh in hs]
    acts, outs = [], []
    for i in range(FFN_SUBTILES + 1):
        if i < FFN_SUBTILES:
            gate = _dot(xs[i], w1_ref[...])
            up = _dot(xs[i], w3_ref[...])
            acts.append((gate * jax.nn.sigmoid(gate) * up).astype(BF16))
        if i >= 1:
            outs.append(hs[i - 1] + FFN_RES_WEIGHT * _dot(acts[i - 1], w2_ref[...]))
    return jnp.concatenate(outs, axis=0)


def _ffn_kernel(*refs, with_attn, with_final):
    refs = list(refs)
    o_ref = refs.pop()
    h = refs.pop(0)[...]
    if with_attn:
        a_ref, wo_ref = refs.pop(0), refs.pop(0)
        h = h + _dot(a_ref[...], wo_ref[...])
    g_ref, w1_ref, w3_ref, w2_ref = refs[:4]
    out = _ffn_body(h, g_ref[...], w1_ref, w3_ref, w2_ref)
    if with_final:
        out = _rms(out, refs[4][...])
    o_ref[...] = out


def _ffn(h, g, w1, w3, w2, attn=None, w_o=None, final_gain=None):
    t, d = h.shape
    row = lambda width: pl.BlockSpec((FFN_ROWS, width), lambda i: (i, 0))
    in_specs, args = [row(d)], [h]
    if attn is not None:
        in_specs += [row(attn.shape[1]), _resident(w_o.shape)]
        args += [attn, w_o]
    in_specs += [_resident((1, d)), _resident(w1.shape), _resident(w3.shape), _resident(w2.shape)]
    args += [g.reshape(1, d), w1, w3, w2]
    if final_gain is not None:
        in_specs.append(_resident((1, d)))
        args.append(final_gain.reshape(1, d))
    return pl.pallas_call(
        functools.partial(_ffn_kernel, with_attn=attn is not None,
                          with_final=final_gain is not None),
        grid=(t // FFN_ROWS,),
        in_specs=in_specs,
        out_specs=row(d),
        out_shape=jax.ShapeDtypeStruct((t, d), F32),
        compiler_params=_params(1),
        name="ffn" + ("_attn" if attn is not None else "") + ("_final" if final_gain is not None else ""),
    )(*args)


def _conv_kernel(h_ref, g_ref, wpw1_ref, wdw_ref, gn_ref, wpw2_ref, o_ref, u_ref, y_ref):
    rows, d = y_ref.shape
    n_lane_tiles = d // LANES

    @pl.when(pl.program_id(1) == 0)
    def _():
        u_ref[:, 0:CONV_TAIL, :] = jnp.zeros((n_lane_tiles, CONV_TAIL, LANES), F32)

    first = CONV_TAIL - (CONV_WIDTH - 1)
    for stage in range(rows // CONV_STAGE_ROWS):
        r_lo = stage * CONV_STAGE_ROWS
        stage_rows = slice(r_lo, r_lo + CONV_STAGE_ROWS)
        h = h_ref[0, stage_rows, :]
        m = _rms(h, g_ref[...]).astype(BF16)
        u = _dot(m, wpw1_ref[...])
        glu = u[:, :d] * jax.nn.sigmoid(u[:, d:])
        for c in range(n_lane_tiles):
            u_ref[c, CONV_TAIL + r_lo:CONV_TAIL + r_lo + CONV_STAGE_ROWS, :] = (
                glu[:, c * LANES:(c + 1) * LANES])

        for c in range(n_lane_tiles):
            lanes = slice(c * LANES, (c + 1) * LANES)
            for r0 in range(r_lo, r_lo + CONV_STAGE_ROWS, CONV_ROW_BLOCK):
                acc = wdw_ref[0:1, lanes] * u_ref[c, r0 + first:r0 + first + CONV_ROW_BLOCK, :]
                for k in range(1, CONV_WIDTH):
                    lo = r0 + first + k
                    acc = acc + wdw_ref[k:k + 1, lanes] * u_ref[c, lo:lo + CONV_ROW_BLOCK, :]
                y_ref[r0:r0 + CONV_ROW_BLOCK, lanes] = acc

        z = _rms(y_ref[stage_rows, :], gn_ref[...])
        z = (z * jax.nn.sigmoid(z)).astype(BF16)
        o_ref[0, stage_rows, :] = h + _dot(z, wpw2_ref[...])

    u_ref[:, 0:CONV_TAIL, :] = u_ref[:, rows:rows + CONV_TAIL, :]


def _conv_module(h, g, w_pw1, w_dw, g_norm, w_pw2):
    b, s, d = h.shape
    row = pl.BlockSpec((1, CONV_ROWS, d), lambda bi, si: (bi, si, 0))
    return pl.pallas_call(
        _conv_kernel,
        grid=(b, s // CONV_ROWS),
        in_specs=[row, _resident((1, d)), _resident((d, 2 * d)), _resident((CONV_WIDTH, d)),
                  _resident((1, d)), _resident((d, d))],
        out_specs=row,
        out_shape=jax.ShapeDtypeStruct((b, s, d), F32),
        scratch_shapes=[pltpu.VMEM((d // LANES, CONV_TAIL + CONV_ROWS, LANES), F32),
                        pltpu.VMEM((CONV_ROWS, d), F32)],
        compiler_params=_params(2),
        name="conv_module",
    )(h, g.reshape(1, d), w_pw1, w_dw, g_norm.reshape(1, d), w_pw2)


def _dot_nt(a, b):
    return lax.dot_general(a, b, (((1,), (1,)), ((), ())), preferred_element_type=F32)


def _mla_proj_kernel(h_ref, pos_ref, freq_ref, g_ref, wa_ref,
                     gq_ref, gkv_ref, wqn_ref, wqr_ref, wqs_ref, wk_ref, wv_ref,
                     qt_ref, k_ref, vt_ref):
    m = _rms(h_ref[0], g_ref[...]).astype(BF16)
    a = _dot(m, wa_ref[...])
    c_q = _rms(a[:, :Q_LORA], gq_ref[...]).astype(BF16)
    c_kv = _rms(a[:, Q_LORA:Q_LORA + KV_LORA], gkv_ref[...]).astype(BF16)
    kr = a[:, Q_LORA + KV_LORA:Q_LORA + KV_LORA + LANES]
    kr_swapped = a[:, Q_LORA + KV_LORA + LANES:]

    ang = freq_ref[...] * pos_ref[0]
    cos, sin = jnp.cos(ang), jnp.sin(ang)
    pad = jnp.zeros((LANES - QK_ROPE, ang.shape[1]), F32)
    cos_q = jnp.concatenate([cos, cos], axis=0)
    sin_q = jnp.concatenate([-sin, sin], axis=0)
    cos_k = jnp.concatenate([cos_q, pad], axis=0).T
    sin_k = jnp.concatenate([sin_q, pad], axis=0).T

    scale = (QK_NOPE + QK_ROPE) ** -0.5 * LOG2_E
    qt_nope = _dot_nt(wqn_ref[...], c_q)
    qt_rope = _dot_nt(wqr_ref[...], c_q)
    qt_rope_swapped = _dot_nt(wqs_ref[...], c_q)
    k_nope = _dot(c_kv, wk_ref[...])
    vt = _dot_nt(wv_ref[...], c_kv)
    k_rot = (kr * cos_k + kr_swapped * sin_k).astype(BF16)
    aug_row = lax.broadcasted_iota(jnp.int32, (V_AUG - V_HEAD, vt.shape[1]), 0)
    sum_rows = jnp.where(aug_row == 0, 1.0, 0.0).astype(BF16)
    q_pad = jnp.zeros((HEAD_PAD - QK_NOPE - QK_ROPE, vt.shape[1]), BF16)
    for hd in range(N_HEADS):
        sl = slice(hd * LANES, (hd + 1) * LANES)
        rope_rows = slice(hd * QK_ROPE, (hd + 1) * QK_ROPE)
        qt_rot = qt_rope[rope_rows, :] * cos_q + qt_rope_swapped[rope_rows, :] * sin_q
        qt_ref[0, hd, 0:QK_NOPE, :] = (qt_nope[sl, :] * scale).astype(BF16)
        qt_ref[0, hd, QK_NOPE:QK_NOPE + QK_ROPE, :] = (qt_rot * scale).astype(BF16)
        qt_ref[0, hd, QK_NOPE + QK_ROPE:HEAD_PAD, :] = q_pad
        k_ref[0, hd, :, 0:LANES] = k_nope[:, sl].astype(BF16)
        k_ref[0, hd, :, LANES:HEAD_PAD] = k_rot
        vt_ref[0, hd, 0, 0:V_HEAD, :] = vt[sl, :].astype(BF16)
        vt_ref[0, hd, 0, V_HEAD:V_AUG, :] = sum_rows


def _rope_tile_weights(w_rope):
    c, g, _ = w_rope.shape
    half = QK_ROPE // 2
    x1, x2 = w_rope[..., :half], w_rope[..., half:]
    pad = jnp.zeros((c, g, LANES - QK_ROPE), w_rope.dtype)
    plain = jnp.concatenate([x1, x2, pad], axis=-1).reshape(c, g * LANES)
    swapped = jnp.concatenate([x2, x1, pad], axis=-1).reshape(c, g * LANES)
    return plain, swapped


def _mla_proj(h, pos, g, w_a, g_q, g_kv, w_uq, w_ukv):
    b, s, d = h.shape
    kr_plain, kr_swapped = _rope_tile_weights(w_a[:, None, Q_LORA + KV_LORA:])
    wa = jnp.concatenate([w_a[:, :Q_LORA + KV_LORA], kr_plain, kr_swapped], axis=1).astype(BF16)
    wqn_t = w_uq[:, :, :QK_NOPE].reshape(Q_LORA, N_HEADS * QK_NOPE).T.astype(BF16)
    half = QK_ROPE // 2
    w_rope = w_uq[:, :, QK_NOPE:]
    w_rope_swapped = jnp.concatenate([w_rope[..., half:], w_rope[..., :half]], axis=-1)
    wqr_t = w_rope.reshape(Q_LORA, N_HEADS * QK_ROPE).T.astype(BF16)
    wqs_t = w_rope_swapped.reshape(Q_LORA, N_HEADS * QK_ROPE).T.astype(BF16)
    wk = w_ukv[:, :, :QK_NOPE].reshape(KV_LORA, N_HEADS * QK_NOPE).astype(BF16)
    wv_t = w_ukv[:, :, QK_NOPE:].reshape(KV_LORA, N_HEADS * V_HEAD).T.astype(BF16)
    inv_freq = ROPE_THETA ** (-2.0 * jnp.arange(QK_ROPE // 2, dtype=F32) / QK_ROPE)
    pos_f = pos.astype(F32)

    assert PROJ_ROWS == ATTN_BLOCK
    row = pl.BlockSpec((1, PROJ_ROWS, d), lambda bi, si: (bi, si, 0))
    return pl.pallas_call(
        _mla_proj_kernel,
        grid=(b, s // PROJ_ROWS),
        in_specs=[row,
                  pl.BlockSpec((1, 1, PROJ_ROWS), lambda bi, si: (bi, 0, si)),
                  _resident((QK_ROPE // 2, 1)), _resident((1, d)),
                  _resident(wa.shape), _resident((1, Q_LORA)), _resident((1, KV_LORA)),
                  _resident(wqn_t.shape), _resident(wqr_t.shape), _resident(wqs_t.shape),
                  _resident(wk.shape), _resident(wv_t.shape)],
        out_specs=[pl.BlockSpec((1, N_HEADS, HEAD_PAD, PROJ_ROWS), lambda bi, si: (bi, 0, 0, si)),
                   pl.BlockSpec((1, N_HEADS, PROJ_ROWS, HEAD_PAD), lambda bi, si: (bi, 0, si, 0)),
                   pl.BlockSpec((1, N_HEADS, 1, V_AUG, PROJ_ROWS),
                                lambda bi, si: (bi, 0, si, 0, 0))],
        out_shape=[jax.ShapeDtypeStruct((b, N_HEADS, HEAD_PAD, s), BF16),
                   jax.ShapeDtypeStruct((b, N_HEADS, s, HEAD_PAD), BF16),
                   jax.ShapeDtypeStruct((b, N_HEADS, s // PROJ_ROWS, V_AUG, PROJ_ROWS), BF16)],
        compiler_params=_params(2),
        name="mla_proj",
    )(h, pos_f[:, None, :], inv_freq.reshape(QK_ROPE // 2, 1), g.reshape(1, d), wa,
      g_q.reshape(1, Q_LORA),
      g_kv.reshape(1, KV_LORA), wqn_t, wqr_t, wqs_t, wk, wv_t)


def _sublane_all(x, op):
    for shift in (4, 2, 1):
        x = op(x, pltpu.roll(x, shift, axis=0))
    return x


def _attn_kernel(qt_ref, k_ref, vt_ref, bias_ref, o_ref, s0_ref, s1_ref, mx0_ref, mx1_ref,
                 m_ref, acc_ref):
    qb, kb = ATTN_Q_BLOCK, ATTN_BLOCK
    n_sub = qb // kb
    qi = pl.program_id(2)
    bufs = ((s0_ref, mx0_ref), (s1_ref, mx1_ref))

    def keys(j):
        return k_ref[0, 0, pl.ds(pl.multiple_of(j * kb, kb), kb), :]

    def slabs(x):
        return x.reshape(x.shape[0] // SUBLANES, SUBLANES, x.shape[1])

    def raw_scores(j, buf, lo=0, hi=qb, masked=False):
        s_ref, mx_ref = buf
        scores = _dot(keys(j), qt_ref[0, 0, :, lo:hi])
        if masked:
            scores = scores + bias_ref[:, lo % kb:lo % kb + hi - lo]
        s_ref[:, lo:hi] = scores
        mx_ref[:, lo:hi] = jnp.max(slabs(s_ref[:, lo:hi]), axis=0)

    def update(j, buf, lo=0, hi=qb):
        s_ref, mx_ref = buf
        m_old = m_ref[:, lo:hi]
        m_new = jnp.maximum(m_old, _sublane_all(mx_ref[:, lo:hi], jnp.maximum))
        alpha = jnp.exp2(m_old - m_new)
        p = jnp.exp2(slabs(s_ref[:, lo:hi]) - m_new[None])
        m_ref[:, lo:hi] = m_new
        pv = _dot(vt_ref[0, 0, j], p.reshape(kb, hi - lo).astype(BF16))
        acc_ref[:, lo:hi] = (slabs(acc_ref[:, lo:hi]) * alpha[None] + slabs(pv)).reshape(pv.shape)

    def trip(i, carry):
        j = n_sub * i
        for t in range(n_sub):
            for lo in range(0, qb, ATTN_COL_CHUNK):
                raw_scores(j + t + 1, bufs[(t + 1) % 2], lo, lo + ATTN_COL_CHUNK)
                update(j + t, bufs[t % 2], lo, lo + ATTN_COL_CHUNK)
        return carry

    acc_ref[...] = jnp.zeros(acc_ref.shape, F32)
    m_ref[...] = jnp.full(m_ref.shape, -jnp.inf, F32)
    raw_scores(0, bufs[0])
    lax.fori_loop(0, qi, trip, 0)

    j = n_sub * qi
    s0_ref[:, 0:kb] = s0_ref[:, 0:kb] + bias_ref[...]
    mx0_ref[:, 0:kb] = jnp.max(slabs(s0_ref[:, 0:kb]), axis=0)
    chunks_per_block = kb // ATTN_COL_CHUNK
    for t in range(n_sub):
        for c in range(t * chunks_per_block, n_sub * chunks_per_block):
            lo = c * ATTN_COL_CHUNK
            if c // chunks_per_block >= t + 1:
                raw_scores(j + t + 1, bufs[(t + 1) % 2], lo, lo + ATTN_COL_CHUNK,
                           masked=c // chunks_per_block == t + 1)
            update(j + t, bufs[t % 2], lo, lo + ATTN_COL_CHUNK)
    o_ref[0] = (acc_ref[0:V_HEAD, :] / acc_ref[V_HEAD:V_HEAD + 1, :]).T.astype(BF16)


def _attention(qt, k, vt):
    b, n_heads, s, _ = k.shape
    assert ATTN_Q_BLOCK % (2 * ATTN_BLOCK) == 0 and ATTN_BLOCK % CHUNK == 0
    chunk_of = jnp.arange(ATTN_BLOCK, dtype=jnp.int32) // CHUNK
    bias = jnp.where(chunk_of[:, None] <= chunk_of[None, :], 0.0, -jnp.inf).astype(F32)
    return pl.pallas_call(
        _attn_kernel,
        grid=(b, n_heads, s // ATTN_Q_BLOCK),
        in_specs=[pl.BlockSpec((1, 1, HEAD_PAD, ATTN_Q_BLOCK), lambda bi, hi, qi: (bi, hi, 0, qi)),
                  pl.BlockSpec((1, 1, s, HEAD_PAD), lambda bi, hi, qi: (bi, hi, 0, 0)),
                  pl.BlockSpec((1, 1, s // ATTN_BLOCK, V_AUG, ATTN_BLOCK),
                               lambda bi, hi, qi: (bi, hi, 0, 0, 0)),
                  _resident((ATTN_BLOCK, ATTN_BLOCK))],
        out_specs=pl.BlockSpec((1, ATTN_Q_BLOCK, V_HEAD), lambda bi, hi, qi: (bi, qi, hi)),
        out_shape=jax.ShapeDtypeStruct((b, s, n_heads * V_HEAD), BF16),
        scratch_shapes=[pltpu.VMEM((ATTN_BLOCK, ATTN_Q_BLOCK), F32),
                        pltpu.VMEM((ATTN_BLOCK, ATTN_Q_BLOCK), F32),
                        pltpu.VMEM((SUBLANES, ATTN_Q_BLOCK), F32),
                        pltpu.VMEM((SUBLANES, ATTN_Q_BLOCK), F32),
                        pltpu.VMEM((SUBLANES, ATTN_Q_BLOCK), F32),
                        pltpu.VMEM((V_AUG, ATTN_Q_BLOCK), F32)],
        compiler_params=_params(3),
        name="mla_attention",
    )(qt, k, vt, bias)


def kernel(x, positions, ffn_norm1, ffn1_w1, ffn1_w3, ffn1_w2, mix_norm, ffn_norm2, ffn2_w1, ffn2_w3, ffn2_w2, conv_w_pw1, conv_w_dw, conv_norm, conv_w_pw2, mla_w_a, mla_q_norm, mla_kv_norm, mla_w_uq, mla_w_ukv, mla_w_o, final_norm):
    b, s, d = x.shape
    depth = ffn_norm1.shape[0]
    n_mixers = 2
    bf = lambda w: w.astype(BF16)
    ffn1_w1, ffn1_w3, ffn1_w2 = bf(ffn1_w1), bf(ffn1_w3), bf(ffn1_w2)
    ffn2_w1, ffn2_w3, ffn2_w2 = bf(ffn2_w1), bf(ffn2_w3), bf(ffn2_w2)
    h = x
    for i in range(depth):
        h = _ffn(h.reshape(b * s, d), ffn_norm1[i], ffn1_w1[i], ffn1_w3[i],
                 ffn1_w2[i]).reshape(b, s, d)
        j = i // n_mixers
        attn, w_o = None, None
        if i % n_mixers == 0:
            h = _conv_module(h, mix_norm[i], bf(conv_w_pw1[j]), conv_w_dw[j], conv_norm[j],
                             bf(conv_w_pw2[j]))
        else:
            qt, k, vt = _mla_proj(h, positions, mix_norm[i], mla_w_a[j], mla_q_norm[j],
                                  mla_kv_norm[j], mla_w_uq[j], mla_w_ukv[j])
            attn, w_o = _attention(qt, k, vt).reshape(b * s, -1), bf(mla_w_o[j])
        last = i == depth - 1
        h = _ffn(h.reshape(b * s, d), ffn_norm2[i], ffn2_w1[i], ffn2_w3[i],
                 ffn2_w2[i], attn=attn, w_o=w_o,
                 final_gain=final_norm if last else None).reshape(b, s, d)
    return h
```
